```python
import jax, jax.numpy as jnp
from jax import lax
import numpy as np

D_MODEL = 1024
BATCH = 8
SEQ = 8192
DEPTH = 1

MIX_WIDTH = D_MODEL
CONV_WIDTH = MIX_WIDTH // 2
CONV_K = 3
RWKV_WIDTH = MIX_WIDTH - CONV_WIDTH
RWKV_HEAD = 64
RWKV_HEADS = RWKV_WIDTH // RWKV_HEAD
DECAY_LORA = 64
AAA_LORA = 64
GATE_LORA = 128
P_CONV = 3 * CONV_WIDTH
P_RWKV = 3 * RWKV_WIDTH + DECAY_LORA + AAA_LORA + GATE_LORA
P_TOTAL = P_CONV + P_RWKV
D_FF = 4 * D_MODEL
N_MOD = 6
EPS = 1e-6
LNX_EPS = 1e-5 * RWKV_HEAD

kernel_name = "hymba_conv_rwkv7_adaln_block"


def rms_norm(x, eps=EPS):
    xf = x.astype(jnp.float32)
    return (xf * lax.rsqrt(jnp.mean(xf * xf, axis=-1, keepdims=True) + eps)).astype(x.dtype)


def modulate(h, shift, scale):
    return h * (1.0 + scale[:, None, :]) + shift[:, None, :]


def token_shift(u):
    return jnp.pad(u, ((0, 0), (1, 0), (0, 0)))[:, :-1]


def short_conv_mixer(p, conv_w):
    b_gate, c_gate, u = jnp.split(p, 3, axis=-1)
    z = c_gate * u
    z = lax.conv_general_dilated(
        z, conv_w, window_strides=(1,), padding=((CONV_K - 1, 0),),
        dimension_numbers=('NWC', 'WIO', 'NWC'), feature_group_count=CONV_WIDTH)
    return b_gate * z


def rwkv7_step(S, inp):
    r_t, w_t, k_t, v_t, a_t, b_t = inp
    sa = jnp.einsum('bhvk,bhk->bhv', S, a_t)
    S = S * w_t[:, :, None, :] + sa[..., None] * b_t[:, :, None, :] + v_t[..., None] * k_t[:, :, None, :]
    y_t = jnp.einsum('bhvk,bhk->bhv', S, r_t)
    return S, y_t


def rwkv7_mixer(p, mu, w0, w2, a0, a2, g2, k_k, k_a, r_k, lnx_g, lnx_b):
    b, t, _ = p.shape
    H, N = RWKV_HEADS, RWKV_HEAD
    p = p + (token_shift(p) - p) * mu
    R = RWKV_WIDTH
    r, k, v, w_lo, a_lo, g_lo = jnp.split(
        p, [R, 2 * R, 3 * R, 3 * R + DECAY_LORA, 3 * R + DECAY_LORA + AAA_LORA], axis=-1)
    w = -jax.nn.softplus(-(w0 + jnp.tanh(w_lo) @ w2)) - 0.5
    a = jax.nn.sigmoid(a0 + a_lo @ a2)
    g = jax.nn.sigmoid(g_lo) @ g2
    heads = lambda z: z.reshape(b, t, H, N).astype(jnp.float32)
    kk = heads(k * k_k)
    kk = kk / jnp.maximum(jnp.sqrt(jnp.sum(kk * kk, axis=-1, keepdims=True)), 1e-12)
    k = k * (1.0 + (a - 1.0) * k_a)
    rh, kh, vh, ah = heads(r), heads(k), heads(v), heads(a)
    decay = jnp.exp(-jnp.exp(heads(w)))
    a_vec = -kk
    b_vec = kk * ah
    seq_first = lambda z: jnp.moveaxis(z, 1, 0)
    S0 = jnp.zeros((b, H, N, N), jnp.float32)
    _, y = lax.scan(rwkv7_step, S0,
                    (seq_first(rh), seq_first(decay), seq_first(kh), seq_first(vh),
                     seq_first(a_vec), seq_first(b_vec)))
    y = jnp.moveaxis(y, 0, 1)
    mean = jnp.mean(y, axis=-1, keepdims=True)
    var = jnp.mean(jnp.square(y - mean), axis=-1, keepdims=True)
    y = ((y - mean) * lax.rsqrt(var + LNX_EPS)).reshape(b, t, R)
    y = y * lnx_g.astype(jnp.float32) + lnx_b.astype(jnp.float32)
    bonus = jnp.sum(rh * kh * r_k.astype(jnp.float32), axis=-1, keepdims=True) * vh
    y = y + bonus.reshape(b, t, R)
    return (y.astype(p.dtype)) * g


def setup_inputs(seed: int = 0) -> dict:
    key = jax.random.key(seed)
    ks = jax.random.split(key, 24)
    nrm = lambda k, shape, s: jax.random.normal(k, shape, jnp.float32) * s
    L = DEPTH
    x = jax.random.normal(ks[0], (BATCH, SEQ, D_MODEL), jnp.float32)
    c = jax.random.normal(ks[1], (BATCH, D_MODEL), jnp.float32)
    w_ada = nrm(ks[2], (L, D_MODEL, N_MOD * D_MODEL), 0.5 * D_MODEL ** -0.5)
    b_ada = nrm(ks[3], (L, N_MOD * D_MODEL), 0.01)
    w_in = nrm(ks[4], (L, D_MODEL, P_TOTAL), D_MODEL ** -0.5)
    conv_w = nrm(ks[5], (L, CONV_K, 1, CONV_WIDTH), CONV_K ** -0.5)
    rwkv_mu = jax.random.uniform(ks[6], (L, P_RWKV), jnp.float32)
    w0 = jax.random.uniform(ks[7], (L, RWKV_WIDTH), jnp.float32, -6.0, 0.0)
    w2 = nrm(ks[8], (L, DECAY_LORA, RWKV_WIDTH), 0.5 * DECAY_LORA ** -0.5)
    a0 = nrm(ks[9], (L, RWKV_WIDTH), 0.1)
    a2 = nrm(ks[10], (L, AAA_LORA, RWKV_WIDTH), AAA_LORA ** -0.5)
    g2 = nrm(ks[11], (L, GATE_LORA, RWKV_WIDTH), GATE_LORA ** -0.5)
    k_k = 0.85 + nrm(ks[12], (L, RWKV_WIDTH), 0.02)
    k_a = 1.0 + nrm(ks[13], (L, RWKV_WIDTH), 0.02)
    r_k = nrm(ks[14], (L, RWKV_HEADS, RWKV_HEAD), 0.1)
    lnx_g = 1.0 + nrm(ks[15], (L, RWKV_WIDTH), 0.02)
    lnx_b = nrm(ks[16], (L, RWKV_WIDTH), 0.01)
    w_out = nrm(ks[17], (L, MIX_WIDTH, D_MODEL), MIX_WIDTH ** -0.5)
    w_up = nrm(ks[18], (L, D_MODEL, D_FF), D_MODEL ** -0.5)
    w_down = nrm(ks[19], (L, D_FF, D_MODEL), D_FF ** -0.5)
    final_g = 1.0 + nrm(ks[20], (D_MODEL,), 0.02)
    return {"x": x, "c": c, "w_ada": w_ada, "b_ada": b_ada, "w_in": w_in,
            "conv_w": conv_w, "rwkv_mu": rwkv_mu, "w0": w0, "w2": w2, "a0": a0,
            "a2": a2, "g2": g2, "k_k": k_k, "k_a": k_a, "r_k": r_k,
            "lnx_g": lnx_g, "lnx_b": lnx_b, "w_out": w_out, "w_up": w_up,
            "w_down": w_down, "final_g": final_g}


def reference(x, c, w_ada, b_ada, w_in, conv_w, rwkv_mu, w0, w2, a0, a2, g2,
              k_k, k_a, r_k, lnx_g, lnx_b, w_out, w_up, w_down, final_g):
    cond = jax.nn.silu(c)
    for l in range(DEPTH):
        mod = cond @ w_ada[l] + b_ada[l]
        sh1, sc1, gt1, sh2, sc2, gt2 = jnp.split(mod, N_MOD, axis=-1)
        h = modulate(rms_norm(x), sh1, sc1)
        p = h @ w_in[l]
        y_conv = short_conv_mixer(p[..., :P_CONV], conv_w[l])
        y_rwkv = rwkv7_mixer(p[..., P_CONV:], rwkv_mu[l], w0[l], w2[l], a0[l], a2[l],
                             g2[l], k_k[l], k_a[l], r_k[l], lnx_g[l], lnx_b[l])
        y = jnp.concatenate([y_conv, y_rwkv], axis=-1) @ w_out[l]
        x = x + gt1[:, None, :] * y
        h = modulate(rms_norm(x), sh2, sc2)
        f = jnp.square(jax.nn.relu(h @ w_up[l])) @ w_down[l]
        x = x + gt2[:, None, :] * f
    return rms_norm(x) * final_g
```

```python
import functools

import jax
import jax.numpy as jnp
from jax import lax
from jax.experimental import pallas as pl
from jax.experimental.pallas import tpu as pltpu

F32 = jnp.float32
BF16 = jnp.bfloat16

HEAD = 64
CHUNK = 64
PAIR = 2 * HEAD
EPS = 1e-6
LNX_EPS = 1e-5 * HEAD
VMEM_LIMIT = 56 * 1024 * 1024


def _dot(a, b):
    return jnp.dot(a.astype(BF16), b.astype(BF16), preferred_element_type=F32)


def _dot_nt(a, b):
    return lax.dot_general(a.astype(BF16), b.astype(BF16), (((1,), (1,)), ((), ())),
                           preferred_element_type=F32)


def _dot_tn(a, b):
    return lax.dot_general(a.astype(BF16), b.astype(BF16), (((0,), (0,)), ((), ())),
                           preferred_element_type=F32)


def _dot_f32(a, b):
    return jnp.dot(a, b, precision=lax.Precision.HIGHEST, preferred_element_type=F32)


def _split2(x):
    hi = x.astype(BF16)
    lo = (x - hi.astype(F32)).astype(BF16)
    return hi, lo


def _headsum(x, bd):
    hi, lo = _split2(x)
    return (jnp.dot(hi, bd, preferred_element_type=F32)
            + jnp.dot(lo, bd, preferred_element_type=F32))


def _softplus(x):
    return jnp.maximum(x, 0.0) + jnp.log1p(jnp.exp(-jnp.abs(x)))


def _rms(x):
    return x * lax.rsqrt(jnp.mean(x * x, axis=-1, keepdims=True) + EPS)


def _mod_kernel(c_ref, w_ref, b_ref, o_ref):
    c = c_ref[...]
    cond = c * jax.nn.sigmoid(c)
    o_ref[...] = _dot_f32(cond, w_ref[...]) + b_ref[...]


def _mod_call(c, w_ada, b_ada):
    bsz, d = c.shape
    n = w_ada.shape[1]
    bn = 1024
    return pl.pallas_call(
        _mod_kernel,
        grid=(n // bn,),
        in_specs=[pl.BlockSpec((bsz, d), lambda j: (0, 0)),
                  pl.BlockSpec((d, bn), lambda j: (0, j)),
                  pl.BlockSpec((1, bn), lambda j: (0, j))],
        out_specs=pl.BlockSpec((bsz, bn), lambda j: (0, j)),
        out_shape=jax.ShapeDtypeStruct((bsz, n), F32),
        compiler_params=pltpu.CompilerParams(dimension_semantics=("arbitrary",),
                                             vmem_limit_bytes=VMEM_LIMIT),
        name="mod",
    )(c, w_ada, b_ada)


def _inproj_kernel(x_ref, mod_ref, win_ref, cw_ref, mu_ref, pv_ref, ww_ref, wa_ref, g2_ref, bd_ref,
                   yc_ref, r_ref, k_ref, v_ref, kk_ref, b_ref, ld_ref, bon_ref, g_ref,
                   zc_ref, qc_ref, *, conv_w, rw):
    tt = x_ref.shape[1]

    @pl.when(pl.program_id(1) == 0)
    def _():
        zc_ref[...] = jnp.zeros_like(zc_ref)
        qc_ref[...] = jnp.zeros_like(qc_ref)

    x = x_ref[0]
    h = _rms(x) * (1.0 + mod_ref[0, 1:2, :]) + mod_ref[0, 0:1, :]
    p = jnp.dot(h.astype(BF16), win_ref[...], preferred_element_type=F32)
    first = lax.broadcasted_iota(jnp.int32, (tt, 1), 0) == 0

    cw = conv_w
    z = p[:, cw:2 * cw] * p[:, 2 * cw:3 * cw]
    z1 = jnp.where(first, zc_ref[0:1, :], pltpu.roll(z, 1, 0))
    z2 = jnp.where(first, zc_ref[1:2, :], pltpu.roll(z1, 1, 0))
    zc_ref[0:1, :] = z[tt - 1:tt, :]
    zc_ref[1:2, :] = z1[tt - 1:tt, :]
    yc = p[:, 0:cw] * (cw_ref[0:1, :] * z2 + cw_ref[1:2, :] * z1 + cw_ref[2:3, :] * z)
    yc_ref[0] = yc.astype(BF16)

    q = p[:, 3 * cw:]
    qprev = jnp.where(first, qc_ref[0:1, :], pltpu.roll(q, 1, 0))
    qc_ref[0:1, :] = q[tt - 1:tt, :]
    q = q + (qprev - q) * mu_ref[...]
    r = q[:, 0:rw]
    k = q[:, rw:2 * rw]
    v = q[:, 2 * rw:3 * rw]
    wa = q[:, 3 * rw:3 * rw + PAIR]
    gl = q[:, 3 * rw + PAIR:]
    lane = lax.broadcasted_iota(jnp.int32, (1, PAIR), 1)
    wa = jnp.where(lane < HEAD, jnp.tanh(wa), wa)
    w_pre = _dot(wa, ww_ref[...])
    a_pre = _dot(wa, wa_ref[...])
    g_ref[0] = _dot(jax.nn.sigmoid(gl), g2_ref[...])
    w0, a0, k_k, k_a, r_k = (pv_ref[i:i + 1, :] for i in range(5))
    ld_ref[0] = -jnp.exp(-_softplus(-(w0 + w_pre)) - 0.5)
    a = jax.nn.sigmoid(a0 + a_pre)
    bd = bd_ref[...]
    kk = k * k_k
    kk = kk / jnp.maximum(jnp.sqrt(_headsum(kk * kk, bd)), 1e-12)
    k2 = k * (1.0 + (a - 1.0) * k_a)
    r_ref[0] = r
    k_ref[0] = k2
    v_ref[0] = v
    kk_ref[0] = kk
    b_ref[0] = kk * a
    bon_ref[0] = _headsum(r * k2 * r_k, bd) * v


def _inproj_call(x, mod, win, cw, mu, pv, ww, wa, g2, bd, *, tt):
    bsz, t, d = x.shape
    ptot = win.shape[1]
    conv_w = cw.shape[1]
    rw = pv.shape[1]
    full = lambda a: pl.BlockSpec(a.shape, lambda b, i: (0,) * a.ndim)
    tok = lambda w: pl.BlockSpec((1, tt, w), lambda b, i: (b, i, 0))
    outs = [jax.ShapeDtypeStruct((bsz, t, conv_w), BF16)] + [jax.ShapeDtypeStruct((bsz, t, rw), F32)] * 8
    return pl.pallas_call(
        functools.partial(_inproj_kernel, conv_w=conv_w, rw=rw),
        grid=(bsz, t // tt),
        in_specs=[tok(d), pl.BlockSpec((1,) + mod.shape[1:], lambda b, i: (b, 0, 0)),
                  full(win), full(cw), full(mu), full(pv), full(ww), full(wa), full(g2), full(bd)],
        out_specs=[tok(conv_w)] + [tok(rw)] * 8,
        out_shape=outs,
        scratch_shapes=[pltpu.VMEM((8, conv_w), F32), pltpu.VMEM((8, ptot - 3 * conv_w), F32)],
        compiler_params=pltpu.CompilerParams(dimension_semantics=("arbitrary", "arbitrary"),
                                             vmem_limit_bytes=VMEM_LIMIT),
        name="inproj",
    )(x, mod, win, cw, mu, pv, ww, wa, g2, bd)


def _rwkv_kernel(r_ref, k_ref, v_ref, kk_ref, b_ref, ld_ref, bon_ref, g_ref, lnx_ref, bd_ref,
                 o_ref, p_scr, y_scr):
    tt = r_ref.shape[1]
    rw = r_ref.shape[2]
    npair = rw // PAIR
    c2 = 2 * CHUNK

    @pl.when(pl.program_id(1) == 0)
    def _():
        p_scr[...] = jnp.zeros_like(p_scr)

    ri = lax.broadcasted_iota(jnp.int32, (c2, c2), 0)
    ci = lax.broadcasted_iota(jnp.int32, (c2, c2), 1)
    same = (ri // CHUNK) == (ci // CHUNK)
    strict = same & ((ri % CHUNK) > (ci % CHUNK))
    incl = same & ((ri % CHUNK) >= (ci % CHUNK))
    eye = ri == ci
    eye_f = jnp.where(eye, 1.0, 0.0).astype(F32)
    li = lax.broadcasted_iota(jnp.int32, (CHUNK, CHUNK), 0)
    lj = lax.broadcasted_iota(jnp.int32, (CHUNK, CHUNK), 1)
    tril = jnp.where(li >= lj, 1.0, 0.0).astype(F32)
    head0 = lax.broadcasted_iota(jnp.int32, (CHUNK, PAIR), 1) < HEAD

    def stack(xp):
        return jnp.concatenate([jnp.where(head0, xp, 0.0), jnp.where(head0, 0.0, xp)], axis=0)

    def unstack(xs):
        return xs[0:CHUNK] + xs[CHUNK:c2]

    def chunk_body(c, carry):
        s = pl.multiple_of(c * CHUNK, CHUNK)
        for p in range(npair):
            sl = (0, pl.ds(s, CHUNK), slice(p * PAIR, (p + 1) * PAIR))
            r, k, v, kk, b, ld = (ref[sl] for ref in (r_ref, k_ref, v_ref, kk_ref, b_ref, ld_ref))
            cum = _dot_f32(tril, ld)
            g_c = jnp.exp(cum[CHUNK - 1:CHUNK, :])
            einv = jnp.exp(-cum)
            rt = r * jnp.exp(cum)
            at = -kk * jnp.exp(cum - ld)
            bt = b * einv
            kt = k * einv
            bh = bt * g_c
            kh = kt * g_c
            ats, rts, vs = stack(at), stack(rt), stack(v)
            gram = _dot_nt(jnp.concatenate([ats, rts], axis=0),
                           jnp.concatenate([stack(bt), stack(kt)], axis=0))
            aab = jnp.where(strict, gram[0:c2, 0:c2], 0.0)
            aak = jnp.where(strict, gram[0:c2, c2:], 0.0)
            rb = jnp.where(incl, gram[c2:, 0:c2], 0.0)
            rk = jnp.where(incl, gram[c2:, c2:], 0.0)
            tinv = eye_f + aab
            apow = aab
            n = 2
            while n < CHUNK:
                apow = _dot(apow, apow)
                tinv = tinv + _dot(apow, tinv)
                n *= 2
            wu = _dot(tinv, jnp.concatenate([ats, _dot(aak, vs)], axis=1))
            rwu = _dot(rb, wu)
            rh = unstack(rts + rwu[:, 0:PAIR])
            yh = unstack(rwu[:, PAIR:] + _dot(rk, vs))
            w2 = unstack(wu[:, 0:PAIR])
            ut2 = unstack(wu[:, PAIR:])
            bw = jnp.where(same, _dot_tn(bh, w2), 0.0)
            nt = jnp.where(same, _dot_tn(bh, ut2) + _dot_tn(kh, v), 0.0)
            g_col = jnp.sum(jnp.where(eye, g_c, 0.0), axis=1, keepdims=True)
            pp = p_scr[p]
            y_scr[pl.ds(s, CHUNK), p * PAIR:(p + 1) * PAIR] = _dot(rh, pp) + yh
            p_scr[p] = g_col * pp + _dot(bw, pp) + nt
        return carry

    lax.fori_loop(0, tt // CHUNK, chunk_body, 0)

    bd = bd_ref[...]
    y = y_scr[...]
    d = y - _headsum(y, bd) * (1.0 / HEAD)
    var = _headsum(d * d, bd) * (1.0 / HEAD)
    yn = d * lax.rsqrt(var + LNX_EPS)
    yn = yn * lnx_ref[0:1, :] + lnx_ref[1:2, :] + bon_ref[0]
    o_ref[0] = (yn * g_ref[0]).astype(BF16)


def _rwkv_call(r, k, v, kk, b, ld, bon, g, lnx, bd, *, tt):
    bsz, t, rw = r.shape
    tok = pl.BlockSpec((1, tt, rw), lambda bi, i: (bi, i, 0))
    full = lambda a: pl.BlockSpec(a.shape, lambda bi, i: (0,) * a.ndim)
    return pl.pallas_call(
        _rwkv_kernel,
        grid=(bsz, t // tt),
        in_specs=[tok] * 8 + [full(lnx), full(bd)],
        out_specs=tok,
        out_shape=jax.ShapeDtypeStruct((bsz, t, rw), BF16),
        scratch_shapes=[pltpu.VMEM((rw // PAIR, PAIR, PAIR), F32), pltpu.VMEM((tt, rw), F32)],
        compiler_params=pltpu.CompilerParams(dimension_semantics=("arbitrary", "arbitrary"),
                                             vmem_limit_bytes=VMEM_LIMIT),
        name="rwkv",
    )(r, k, v, kk, b, ld, bon, g, lnx, bd)


def _outmlp_kernel(x_ref, yc_ref, yr_ref, mod_ref, wo_ref, wup_ref, wdn_ref, fg_ref, o_ref, *, ff_block):
    cw = yc_ref.shape[2]
    y = (jnp.dot(yc_ref[0], wo_ref[0:cw, :], preferred_element_type=F32)
         + jnp.dot(yr_ref[0], wo_ref[cw:, :], preferred_element_type=F32))
    x1 = x_ref[0] + mod_ref[0, 2:3, :] * y
    h = (_rms(x1) * (1.0 + mod_ref[0, 4:5, :]) + mod_ref[0, 3:4, :]).astype(BF16)
    f = jnp.zeros_like(x1)
    for j in range(wup_ref.shape[1] // ff_block):
        hj = jnp.dot(h, wup_ref[:, j * ff_block:(j + 1) * ff_block], preferred_element_type=F32)
        hj = jnp.square(jnp.maximum(hj, 0.0)).astype(BF16)
        f = f + jnp.dot(hj, wdn_ref[j * ff_block:(j + 1) * ff_block, :], preferred_element_type=F32)
    x2 = x1 + mod_ref[0, 5:6, :] * f
    o_ref[0] = _rms(x2) * fg_ref[...]


def _outmlp_call(x, yc, yr, mod, wo, wup, wdn, fg, *, tt):
    bsz, t, d = x.shape
    tok = lambda w: pl.BlockSpec((1, tt, w), lambda b, i: (b, i, 0))
    res = lambda a: pl.BlockSpec(a.shape, lambda b, i: (0,) * a.ndim, pipeline_mode=pl.Buffered(1))
    return pl.pallas_call(
        functools.partial(_outmlp_kernel, ff_block=1024),
        grid=(bsz, t // tt),
        in_specs=[tok(d), tok(yc.shape[2]), tok(yr.shape[2]),
                  pl.BlockSpec((1,) + mod.shape[1:], lambda b, i: (b, 0, 0)),
                  res(wo), res(wup), res(wdn), res(fg)],
        out_specs=tok(d),
        out_shape=jax.ShapeDtypeStruct((bsz, t, d), F32),
        compiler_params=pltpu.CompilerParams(dimension_semantics=("arbitrary", "arbitrary"),
                                             vmem_limit_bytes=VMEM_LIMIT),
        name="outmlp",
    )(x, yc, yr, mod, wo, wup, wdn, fg)


def kernel(x, c, w_ada, b_ada, w_in, conv_w, rwkv_mu, w0, w2, a0, a2, g2, k_k, k_a, r_k,
           lnx_g, lnx_b, w_out, w_up, w_down, final_g):
    assert w_ada.shape[0] == 1, "single-layer block"
    bsz, t, d = x.shape
    rw = w0.shape[1]
    cw = conv_w.shape[3]
    n_mod = w_ada.shape[2] // d

    mod = _mod_call(c, w_ada[0], b_ada).reshape(bsz, n_mod, d)

    zpad = jnp.zeros((PAIR - w2.shape[1], rw), F32)
    ww = jnp.concatenate([w2[0], zpad], axis=0).astype(BF16)
    wa = jnp.concatenate([zpad, a2[0]], axis=0).astype(BF16)
    pv = jnp.concatenate([w0, a0, k_k, k_a, r_k.reshape(1, rw), jnp.zeros((3, rw), F32)], axis=0)
    hid = jnp.arange(rw, dtype=jnp.int32) // HEAD
    bd = (hid[:, None] == hid[None, :]).astype(BF16)
    lnx = jnp.concatenate([lnx_g, lnx_b], axis=0)

    yc, r, k2, v, kk, b, ld, bon, g = _inproj_call(
        x, mod, w_in[0].astype(BF16), conv_w[0, :, 0, :], rwkv_mu, pv, ww, wa,
        g2[0].astype(BF16), bd, tt=256)
    yr = _rwkv_call(r, k2, v, kk, b, ld, bon, g, lnx, bd, tt=256)
    return _outmlp_call(x, yc, yr, mod, w_out[0].astype(BF16), w_up[0].astype(BF16),
                        w_down[0].astype(BF16), final_g.reshape(1, d), tt=512)
```

```python
import functools

import jax
import jax.numpy as jnp
from jax import lax
from jax.experimental import pallas as pl
from jax.experimental.pallas import tpu as pltpu

F32 = jnp.float32
BF16 = jnp.bfloat16

HEAD = 64
CHUNK = 64
PAIR = 2 * HEAD
QUAD = 4 * HEAD
EPS = 1e-6
LNX_EPS = 1e-5 * HEAD
VMEM_LIMIT = 56 * 1024 * 1024


def _dot(a, b):
    return jnp.dot(a.astype(BF16), b.astype(BF16), preferred_element_type=F32)


def _dot_nt(a, b):
    return lax.dot_general(a.astype(BF16), b.astype(BF16), (((1,), (1,)), ((), ())),
                           preferred_element_type=F32)


def _dot_tn(a, b):
    return lax.dot_general(a.astype(BF16), b.astype(BF16), (((0,), (0,)), ((), ())),
                           preferred_element_type=F32)


def _dot_f32(a, b):
    return jnp.dot(a, b, precision=lax.Precision.HIGHEST, preferred_element_type=F32)


def _split2(x):
    hi = x.astype(BF16)
    lo = (x - hi.astype(F32)).astype(BF16)
    return hi, lo


def _headsum(x, bd):
    hi, lo = _split2(x)
    return (jnp.dot(hi, bd, preferred_element_type=F32)
            + jnp.dot(lo, bd, preferred_element_type=F32))


def _softplus(x):
    return jnp.maximum(x, 0.0) + jnp.log1p(jnp.exp(-jnp.abs(x)))


def _rms(x):
    return x * lax.rsqrt(jnp.mean(x * x, axis=-1, keepdims=True) + EPS)


def _mod_kernel(c_ref, w_ref, b_ref, o_ref):
    c = c_ref[...]
    cond = c * jax.nn.sigmoid(c)
    o_ref[...] = _dot_f32(cond, w_ref[...]) + b_ref[...]


def _mod_call(c, w_ada, b_ada):
    bsz, d = c.shape
    n = w_ada.shape[1]
    bn = 1024
    return pl.pallas_call(
        _mod_kernel,
        grid=(n // bn,),
        in_specs=[pl.BlockSpec((bsz, d), lambda j: (0, 0)),
                  pl.BlockSpec((d, bn), lambda j: (0, j)),
                  pl.BlockSpec((1, bn), lambda j: (0, j))],
        out_specs=pl.BlockSpec((bsz, bn), lambda j: (0, j)),
        out_shape=jax.ShapeDtypeStruct((bsz, n), F32),
        compiler_params=pltpu.CompilerParams(dimension_semantics=("arbitrary",),
                                             vmem_limit_bytes=VMEM_LIMIT),
        name="mod",
    )(c, w_ada, b_ada)


def _inproj_kernel(x_ref, mod_ref, win_ref, cw_ref, mu_ref, pv_ref, ww_ref, wa_ref, g2_ref, bd_ref,
                   yc_ref, r_ref, k_ref, v_ref, kk_ref, b_ref, ld_ref, bon_ref, g_ref,
                   zc_ref, qc_ref, *, conv_w, rw):
    tt = x_ref.shape[1]

    @pl.when(pl.program_id(1) == 0)
    def _():
        zc_ref[...] = jnp.zeros_like(zc_ref)
        qc_ref[...] = jnp.zeros_like(qc_ref)

    x = x_ref[0]
    h = _rms(x) * (1.0 + mod_ref[0, 1:2, :]) + mod_ref[0, 0:1, :]
    p = jnp.dot(h.astype(BF16), win_ref[...], preferred_element_type=F32)
    first = lax.broadcasted_iota(jnp.int32, (tt, 1), 0) == 0

    cw = conv_w
    z = p[:, cw:2 * cw] * p[:, 2 * cw:3 * cw]
    z1 = jnp.where(first, zc_ref[0:1, :], pltpu.roll(z, 1, 0))
    z2 = jnp.where(first, zc_ref[1:2, :], pltpu.roll(z1, 1, 0))
    zc_ref[0:1, :] = z[tt - 1:tt, :]
    zc_ref[1:2, :] = z1[tt - 1:tt, :]
    yc = p[:, 0:cw] * (cw_ref[0:1, :] * z2 + cw_ref[1:2, :] * z1 + cw_ref[2:3, :] * z)
    yc_ref[0] = yc.astype(BF16)

    q = p[:, 3 * cw:]
    qprev = jnp.where(first, qc_ref[0:1, :], pltpu.roll(q, 1, 0))
    qc_ref[0:1, :] = q[tt - 1:tt, :]
    q = q + (qprev - q) * mu_ref[...]
    r = q[:, 0:rw]
    k = q[:, rw:2 * rw]
    v = q[:, 2 * rw:3 * rw]
    wa = q[:, 3 * rw:3 * rw + PAIR]
    gl = q[:, 3 * rw + PAIR:]
    lane = lax.broadcasted_iota(jnp.int32, (1, PAIR), 1)
    wa = jnp.where(lane < HEAD, jnp.tanh(wa), wa)
    w_pre = _dot(wa, ww_ref[...])
    a_pre = _dot(wa, wa_ref[...])
    g_ref[0] = _dot(jax.nn.sigmoid(gl), g2_ref[...])
    w0, a0, k_k, k_a, r_k = (pv_ref[i:i + 1, :] for i in range(5))
    ld_ref[0] = -jnp.exp(-_softplus(-(w0 + w_pre)) - 0.5)
    a = jax.nn.sigmoid(a0 + a_pre)
    bd = bd_ref[...]
    kk = k * k_k
    kk = kk / jnp.maximum(jnp.sqrt(_headsum(kk * kk, bd)), 1e-12)
    k2 = k * (1.0 + (a - 1.0) * k_a)
    r_ref[0] = r
    k_ref[0] = k2
    v_ref[0] = v
    kk_ref[0] = kk
    b_ref[0] = kk * a
    bon_ref[0] = _headsum(r * k2 * r_k, bd) * v


def _inproj_call(x, mod, win, cw, mu, pv, ww, wa, g2, bd, *, tt):
    bsz, t, d = x.shape
    ptot = win.shape[1]
    conv_w = cw.shape[1]
    rw = pv.shape[1]
    full = lambda a: pl.BlockSpec(a.shape, lambda b, i: (0,) * a.ndim)
    tok = lambda w: pl.BlockSpec((1, tt, w), lambda b, i: (b, i, 0))
    outs = [jax.ShapeDtypeStruct((bsz, t, conv_w), BF16)] + [jax.ShapeDtypeStruct((bsz, t, rw), F32)] * 8
    return pl.pallas_call(
        functools.partial(_inproj_kernel, conv_w=conv_w, rw=rw),
        grid=(bsz, t // tt),
        in_specs=[tok(d), pl.BlockSpec((1,) + mod.shape[1:], lambda b, i: (b, 0, 0)),
                  full(win), full(cw), full(mu), full(pv), full(ww), full(wa), full(g2), full(bd)],
        out_specs=[tok(conv_w)] + [tok(rw)] * 8,
        out_shape=outs,
        scratch_shapes=[pltpu.VMEM((8, conv_w), F32), pltpu.VMEM((8, ptot - 3 * conv_w), F32)],
        compiler_params=pltpu.CompilerParams(dimension_semantics=("arbitrary", "arbitrary"),
                                             vmem_limit_bytes=VMEM_LIMIT),
        name="inproj",
    )(x, mod, win, cw, mu, pv, ww, wa, g2, bd)


def _rwkv_kernel(r_ref, k_ref, v_ref, kk_ref, b_ref, ld_ref, bon_ref, g_ref, lnx_ref, bd_ref,
                 o_ref, p_scr, y_scr):
    tt = r_ref.shape[1]
    rw = r_ref.shape[2]
    nquad = rw // QUAD
    nheads = QUAD // HEAD

    @pl.when(pl.program_id(1) == 0)
    def _():
        p_scr[...] = jnp.zeros_like(p_scr)

    ri = lax.broadcasted_iota(jnp.int32, (tt, tt), 0)
    ci = lax.broadcasted_iota(jnp.int32, (tt, tt), 1)
    same_chunk = (ri // CHUNK) == (ci // CHUNK)
    ld = ld_ref[0]
    cum = _dot_f32(jnp.where(same_chunk & (ri >= ci), 1.0, 0.0).astype(F32), ld)
    tot = _dot_f32(jnp.where(same_chunk, 1.0, 0.0).astype(F32), ld)
    einv = jnp.exp(-cum)
    g_c = jnp.exp(tot)
    rt_all = r_ref[0] * jnp.exp(cum)
    at_all = -kk_ref[0] * jnp.exp(cum - ld)
    bt_all = b_ref[0] * einv
    kt_all = k_ref[0] * einv
    bh_all = bt_all * g_c
    kh_all = kt_all * g_c
    v_all = v_ref[0]

    ti = lax.broadcasted_iota(jnp.int32, (CHUNK, QUAD), 0)
    li = lax.broadcasted_iota(jnp.int32, (CHUNK, QUAD), 1)
    strict = ti > (li % HEAD)
    incl = ti >= (li % HEAD)
    eye_cat = jnp.where(ti == (li % HEAD), 1.0, 0.0).astype(F32)
    head_of_lane = li // HEAD
    qi = lax.broadcasted_iota(jnp.int32, (QUAD, QUAD), 0)
    qj = lax.broadcasted_iota(jnp.int32, (QUAD, QUAD), 1)
    same_head = (qi // HEAD) == (qj // HEAD)
    eye_bd = qi == qj

    def bd_of(x):
        xb = x.astype(BF16)
        z = jnp.zeros_like(xb)
        return jnp.concatenate([jnp.where(head_of_lane == h, xb, z) for h in range(nheads)], axis=0)

    units = [(c, q) for c in range(tt // CHUNK) for q in range(nquad)]

    def blk(x, u):
        c, q = u
        return x[c * CHUNK:(c + 1) * CHUNK, q * QUAD:(q + 1) * QUAD]

    gram = [_dot_nt(jnp.concatenate([blk(at_all, u), blk(rt_all, u)], axis=0),
                    jnp.concatenate([bd_of(blk(bt_all, u)), bd_of(blk(kt_all, u))], axis=0))
            for u in units]
    aab = [jnp.where(strict, g[0:CHUNK, 0:QUAD], 0.0) for g in gram]
    aak = [jnp.where(strict, g[0:CHUNK, QUAD:], 0.0) for g in gram]
    rb = [jnp.where(incl, g[CHUNK:, 0:QUAD], 0.0) for g in gram]
    rk = [jnp.where(incl, g[CHUNK:, QUAD:], 0.0) for g in gram]
    tinv = [eye_cat + a for a in aab]
    apow = aab
    n = 2
    while n < CHUNK:
        apow = [_dot(a, bd_of(a)) for a in apow]
        tinv = [t + _dot(a, bd_of(t)) for a, t in zip(apow, tinv)]
        n *= 2
    vs = [bd_of(blk(v_all, u)) for u in units]
    akv = [_dot(a, x) for a, x in zip(aak, vs)]
    wu = [_dot(t, jnp.concatenate([bd_of(blk(at_all, u)), bd_of(x)], axis=1))
          for t, x, u in zip(tinv, akv, units)]
    rwu = [_dot(m, jnp.concatenate([bd_of(x[:, 0:QUAD]), bd_of(x[:, QUAD:])], axis=1))
           for m, x in zip(rb, wu)]
    rh = [blk(rt_all, u) + x[:, 0:QUAD] for x, u in zip(rwu, units)]
    yh = [x[:, QUAD:] + _dot(m, s) for x, m, s in zip(rwu, rk, vs)]
    bw = [jnp.where(same_head, _dot_tn(blk(bh_all, u), x[:, 0:QUAD]), 0.0) for x, u in zip(wu, units)]
    nt = [jnp.where(same_head,
                    _dot_tn(jnp.concatenate([blk(bh_all, u), blk(kh_all, u)], axis=0),
                            jnp.concatenate([x[:, QUAD:], blk(v_all, u)], axis=0)), 0.0)
          for x, u in zip(wu, units)]
    g_col = [jnp.sum(jnp.where(eye_bd, blk(g_c, u)[0:1, :], 0.0), axis=1, keepdims=True) for u in units]

    state = [p_scr[q] for q in range(nquad)]
    for i, (c, q) in enumerate(units):
        pp = state[q]
        y_scr[c * CHUNK:(c + 1) * CHUNK, q * QUAD:(q + 1) * QUAD] = _dot(rh[i], pp) + yh[i]
        state[q] = g_col[i] * pp + _dot(bw[i], pp) + nt[i]
    for q in range(nquad):
        p_scr[q] = state[q]

    bd = bd_ref[...]
    y = y_scr[...]
    d = y - _headsum(y, bd) * (1.0 / HEAD)
    var = _headsum(d * d, bd) * (1.0 / HEAD)
    yn = d * lax.rsqrt(var + LNX_EPS)
    yn = yn * lnx_ref[0:1, :] + lnx_ref[1:2, :] + bon_ref[0]
    o_ref[0] = (yn * g_ref[0]).astype(BF16)


def _rwkv_call(r, k, v, kk, b, ld, bon, g, lnx, bd, *, tt):
    bsz, t, rw = r.shape
    tok = pl.BlockSpec((1, tt, rw), lambda bi, i: (bi, i, 0))
    full = lambda a: pl.BlockSpec(a.shape, lambda bi, i: (0,) * a.ndim)
    return pl.pallas_call(
        _rwkv_kernel,
        grid=(bsz, t // tt),
        in_specs=[tok] * 8 + [full(lnx), full(bd)],
        out_specs=tok,
        out_shape=jax.ShapeDtypeStruct((bsz, t, rw), BF16),
        scratch_shapes=[pltpu.VMEM((rw // QUAD, QUAD, QUAD), F32), pltpu.VMEM((tt, rw), F32)],
        compiler_params=pltpu.CompilerParams(dimension_semantics=("arbitrary", "arbitrary"),
                                             vmem_limit_bytes=VMEM_LIMIT),
        name="rwkv",
    )(r, k, v, kk, b, ld, bon, g, lnx, bd)


def _outmlp_kernel(x_ref, yc_ref, yr_ref, mod_ref, wo_ref, wup_ref, wdn_ref, fg_ref, o_ref, *, ff_block):
    cw = yc_ref.shape[2]
    y = (jnp.dot(yc_ref[0], wo_ref[0:cw, :], preferred_element_type=F32)
         + jnp.dot(yr_ref[0], wo_ref[cw:, :], preferred_element_type=F32))
    x1 = x_ref[0] + mod_ref[0, 2:3, :] * y
    h = (_rms(x1) * (1.0 + mod_ref[0, 4:5, :]) + mod_ref[0, 3:4, :]).astype(BF16)
    f = jnp.zeros_like(x1)
    for j in range(wup_ref.shape[1] // ff_block):
        hj = jnp.dot(h, wup_ref[:, j * ff_block:(j + 1) * ff_block], preferred_element_type=F32)
        hj = jnp.square(jnp.maximum(hj, 0.0)).astype(BF16)
        f = f + jnp.dot(hj, wdn_ref[j * ff_block:(j + 1) * ff_block, :], preferred_element_type=F32)
    x2 = x1 + mod_ref[0, 5:6, :] * f
    o_ref[0] = _rms(x2) * fg_ref[...]


def _outmlp_call(x, yc, yr, mod, wo, wup, wdn, fg, *, tt):
    bsz, t, d = x.shape
    tok = lambda w: pl.BlockSpec((1, tt, w), lambda b, i: (b, i, 0))
    res = lambda a: pl.BlockSpec(a.shape, lambda b, i: (0,) * a.ndim, pipeline_mode=pl.Buffered(1))
    return pl.pallas_call(
        functools.partial(_outmlp_kernel, ff_block=1024),
        grid=(bsz, t // tt),
        in_specs=[tok(d), tok(yc.shape[2]), tok(yr.shape[2]),
                  pl.BlockSpec((1,) + mod.shape[1:], lambda b, i: (b, 0, 0)),
                  res(wo), res(wup), res(wdn), res(fg)],
        out_specs=tok(d),
        out_shape=jax.ShapeDtypeStruct((bsz, t, d), F32),
        compiler_params=pltpu.CompilerParams(dimension_semantics=("arbitrary", "arbitrary"),
                                             vmem_limit_bytes=VMEM_LIMIT),
        name="outmlp",
    )(x, yc, yr, mod, wo, wup, wdn, fg)


def kernel(x, c, w_ada, b_ada, w_in, conv_w, rwkv_mu, w0, w2, a0, a2, g2, k_k, k_a, r_k,
           lnx_g, lnx_b, w_out, w_up, w_down, final_g):
    assert w_ada.shape[0] == 1, "single-layer block"
    bsz, t, d = x.shape
    rw = w0.shape[1]
    n_mod = w_ada.shape[2] // d

    mod = _mod_call(c, w_ada[0], b_ada).reshape(bsz, n_mod, d)

    zpad = jnp.zeros((PAIR - w2.shape[1], rw), F32)
    ww = jnp.concatenate([w2[0], zpad], axis=0).astype(BF16)
    wa = jnp.concatenate([zpad, a2[0]], axis=0).astype(BF16)
    pv = jnp.concatenate([w0, a0, k_k, k_a, r_k.reshape(1, rw), jnp.zeros((3, rw), F32)], axis=0)
    hid = jnp.arange(rw, dtype=jnp.int32) // HEAD
    bd = (hid[:, None] == hid[None, :]).astype(BF16)
    lnx = jnp.concatenate([lnx_g, lnx_b], axis=0)

    yc, r, k2, v, kk, b, ld, bon, g = _inproj_call(
        x, mod, w_in[0].astype(BF16), conv_w[0, :, 0, :], rwkv_mu, pv, ww, wa,
        g2[0].astype(BF16), bd, tt=256)
    yr = _rwkv_call(r, k2, v, kk, b, ld, bon, g, lnx, bd, tt=256)
    return _outmlp_call(x, yc, yr, mod, w_out[0].astype(BF16), w_up[0].astype(BF16),
                        w_down[0].astype(BF16), final_g.reshape(1, d), tt=512)
```

```python
import functools

import jax
import jax.numpy as jnp
from jax import lax
from jax.experimental import pallas as pl
from jax.experimental.pallas import tpu as pltpu

F32 = jnp.float32
BF16 = jnp.bfloat16

HEAD = 64
CHUNK = 64
PAIR = 2 * HEAD
QUAD = 4 * HEAD
EPS = 1e-6
LNX_EPS = 1e-5 * HEAD
VMEM_LIMIT = 56 * 1024 * 1024


def _dot(a, b):
    return jnp.dot(a.astype(BF16), b.astype(BF16), preferred_element_type=F32)


def _dot_nt(a, b):
    return lax.dot_general(a.astype(BF16), b.astype(BF16), (((1,), (1,)), ((), ())),
                           preferred_element_type=F32)


def _dot_tn(a, b):
    return lax.dot_general(a.astype(BF16), b.astype(BF16), (((0,), (0,)), ((), ())),
                           preferred_element_type=F32)


def _dot_f32(a, b):
    return jnp.dot(a, b, precision=lax.Precision.HIGHEST, preferred_element_type=F32)


def _split2(x):
    hi = x.astype(BF16)
    lo = (x - hi.astype(F32)).astype(BF16)
    return hi, lo


def _split3(x):
    hi = x.astype(BF16)
    r1 = x - hi.astype(F32)
    mid = r1.astype(BF16)
    lo = (r1 - mid.astype(F32)).astype(BF16)
    return hi, mid, lo


def _headsum(x, bd):
    hi, lo = _split2(x)
    return (jnp.dot(hi, bd, preferred_element_type=F32)
            + jnp.dot(lo, bd, preferred_element_type=F32))


def _headsum1(x, bd):
    return jnp.dot(x.astype(BF16), bd, preferred_element_type=F32)


def _softplus(x):
    return jnp.maximum(x, 0.0) + jnp.log1p(jnp.exp(-jnp.abs(x)))


def _rms(x):
    return x * lax.rsqrt(jnp.mean(x * x, axis=-1, keepdims=True) + EPS)


def _mod_kernel(c_ref, w_ref, b_ref, o_ref):
    c = c_ref[...]
    cond = c * jax.nn.sigmoid(c)
    o_ref[...] = _dot_f32(cond, w_ref[...]) + b_ref[...]


def _mod_call(c, w_ada, b_ada):
    bsz, d = c.shape
    n = w_ada.shape[1]
    bn = 1024
    return pl.pallas_call(
        _mod_kernel,
        grid=(n // bn,),
        in_specs=[pl.BlockSpec((bsz, d), lambda j: (0, 0)),
                  pl.BlockSpec((d, bn), lambda j: (0, j)),
                  pl.BlockSpec((1, bn), lambda j: (0, j))],
        out_specs=pl.BlockSpec((bsz, bn), lambda j: (0, j)),
        out_shape=jax.ShapeDtypeStruct((bsz, n), F32),
        compiler_params=pltpu.CompilerParams(dimension_semantics=("arbitrary",),
                                             vmem_limit_bytes=VMEM_LIMIT),
        name="mod",
    )(c, w_ada, b_ada)


def _inproj_kernel(x_ref, mod_ref, win_ref, cw_ref, mu_ref, pv_ref, ww_ref, wa_ref, g2_ref, bd_ref,
                   yc_ref, r_ref, k_ref, v_ref, kk_ref, b_ref, ld_ref, bon_ref, g_ref,
                   zc_ref, qc_ref, *, conv_w, rw):
    tt = x_ref.shape[1]

    @pl.when(pl.program_id(1) == 0)
    def _():
        zc_ref[...] = jnp.zeros_like(zc_ref)
        qc_ref[...] = jnp.zeros_like(qc_ref)

    x = x_ref[0]
    h = _rms(x) * (1.0 + mod_ref[0, 1:2, :]) + mod_ref[0, 0:1, :]
    p = jnp.dot(h.astype(BF16), win_ref[...], preferred_element_type=F32)
    first = lax.broadcasted_iota(jnp.int32, (tt, 1), 0) == 0

    cw = conv_w
    z = p[:, cw:2 * cw] * p[:, 2 * cw:3 * cw]
    z1 = jnp.where(first, zc_ref[0:1, :], pltpu.roll(z, 1, 0))
    z2 = jnp.where(first, zc_ref[1:2, :], pltpu.roll(z1, 1, 0))
    zc_ref[0:1, :] = z[tt - 1:tt, :]
    zc_ref[1:2, :] = z1[tt - 1:tt, :]
    yc = p[:, 0:cw] * (cw_ref[0:1, :] * z2 + cw_ref[1:2, :] * z1 + cw_ref[2:3, :] * z)
    yc_ref[0] = yc.astype(BF16)

    q = p[:, 3 * cw:]
    qprev = jnp.where(first, qc_ref[0:1, :], pltpu.roll(q, 1, 0))
    qc_ref[0:1, :] = q[tt - 1:tt, :]
    q = q + (qprev - q) * mu_ref[...]
    r = q[:, 0:rw]
    k = q[:, rw:2 * rw]
    v = q[:, 2 * rw:3 * rw]
    wa = q[:, 3 * rw:3 * rw + PAIR]
    gl = q[:, 3 * rw + PAIR:]
    lane = lax.broadcasted_iota(jnp.int32, (1, PAIR), 1)
    wa = jnp.where(lane < HEAD, jnp.tanh(wa), wa)
    w_pre = _dot(wa, ww_ref[...])
    a_pre = _dot(wa, wa_ref[...])
    g_ref[0] = _dot(jax.nn.sigmoid(gl), g2_ref[...])
    w0, a0, k_k, k_a, r_k = (pv_ref[i:i + 1, :] for i in range(5))
    ld_ref[0] = -jnp.exp(-_softplus(-(w0 + w_pre)) - 0.5)
    a = jax.nn.sigmoid(a0 + a_pre)
    bd = bd_ref[...]
    kk = k * k_k
    kk = kk / jnp.maximum(jnp.sqrt(_headsum1(kk * kk, bd)), 1e-12)
    k2 = k * (1.0 + (a - 1.0) * k_a)
    r_ref[0] = r
    k_ref[0] = k2
    v_ref[0] = v
    kk_ref[0] = kk
    b_ref[0] = kk * a
    bon_ref[0] = _headsum(r * k2 * r_k, bd) * v


def _inproj_call(x, mod, win, cw, mu, pv, ww, wa, g2, bd, *, tt):
    bsz, t, d = x.shape
    ptot = win.shape[1]
    conv_w = cw.shape[1]
    rw = pv.shape[1]
    full = lambda a: pl.BlockSpec(a.shape, lambda b, i: (0,) * a.ndim)
    tok = lambda w: pl.BlockSpec((1, tt, w), lambda b, i: (b, i, 0))
    outs = [jax.ShapeDtypeStruct((bsz, t, conv_w), BF16)] + [jax.ShapeDtypeStruct((bsz, t, rw), F32)] * 8
    return pl.pallas_call(
        functools.partial(_inproj_kernel, conv_w=conv_w, rw=rw),
        grid=(bsz, t // tt),
        in_specs=[tok(d), pl.BlockSpec((1,) + mod.shape[1:], lambda b, i: (b, 0, 0)),
                  full(win), full(cw), full(mu), full(pv), full(ww), full(wa), full(g2), full(bd)],
        out_specs=[tok(conv_w)] + [tok(rw)] * 8,
        out_shape=outs,
        scratch_shapes=[pltpu.VMEM((8, conv_w), F32), pltpu.VMEM((8, ptot - 3 * conv_w), F32)],
        compiler_params=pltpu.CompilerParams(dimension_semantics=("arbitrary", "arbitrary"),
                                             vmem_limit_bytes=VMEM_LIMIT),
        name="inproj",
    )(x, mod, win, cw, mu, pv, ww, wa, g2, bd)


def _rwkv_kernel(r_ref, k_ref, v_ref, kk_ref, b_ref, ld_ref, bon_ref, g_ref, lnx_ref, bd_ref,
                 o_ref, p_scr, y_scr):
    tt = r_ref.shape[1]
    rw = r_ref.shape[2]
    nquad = rw // QUAD
    nheads = QUAD // HEAD

    @pl.when(pl.program_id(1) == 0)
    def _():
        p_scr[...] = jnp.zeros_like(p_scr)

    ri = lax.broadcasted_iota(jnp.int32, (tt, tt), 0)
    ci = lax.broadcasted_iota(jnp.int32, (tt, tt), 1)
    same_chunk = (ri // CHUNK) == (ci // CHUNK)
    ld = ld_ref[0]
    tril = jnp.where(same_chunk & (ri >= ci), 1.0, 0.0).astype(BF16)
    cum = sum(jnp.dot(tril, part, preferred_element_type=F32) for part in _split3(ld))
    tot = jnp.concatenate(
        [jnp.broadcast_to(cum[c * CHUNK + CHUNK - 1:(c + 1) * CHUNK, :], (CHUNK, rw))
         for c in range(tt // CHUNK)], axis=0)
    einv = jnp.exp(-cum)
    g_c = jnp.exp(tot)
    rt_all = r_ref[0] * jnp.exp(cum)
    at_all = -kk_ref[0] * jnp.exp(cum - ld)
    bt_all = b_ref[0] * einv
    kt_all = k_ref[0] * einv
    bh_all = bt_all * g_c
    kh_all = kt_all * g_c
    v_all = v_ref[0]

    ti = lax.broadcasted_iota(jnp.int32, (CHUNK, QUAD), 0)
    li = lax.broadcasted_iota(jnp.int32, (CHUNK, QUAD), 1)
    strict = ti > (li % HEAD)
    incl = ti >= (li % HEAD)
    eye_cat = jnp.where(ti == (li % HEAD), 1.0, 0.0).astype(F32)
    eye_cat_b = eye_cat.astype(BF16)
    head_of_lane = li // HEAD
    qi = lax.broadcasted_iota(jnp.int32, (QUAD, QUAD), 0)
    qj = lax.broadcasted_iota(jnp.int32, (QUAD, QUAD), 1)
    eye_bd = qi == qj

    def bd_of(x):
        xb = x.astype(BF16)
        z = jnp.zeros_like(xb)
        return jnp.concatenate([jnp.where(head_of_lane == h, xb, z) for h in range(nheads)], axis=0)

    def head_t(xbd):
        return lax.dot_general(eye_cat_b, xbd, (((1,), (1,)), ((), ())), preferred_element_type=F32)

    def col_to_cat(col):
        out = col[0:HEAD]
        for h in range(1, nheads):
            out = jnp.where(head_of_lane == h, col[h * HEAD:(h + 1) * HEAD], out)
        return out

    units = [(c, q) for c in range(tt // CHUNK) for q in range(nquad)]

    def blk(x, u):
        c, q = u
        return x[c * CHUNK:(c + 1) * CHUNK, q * QUAD:(q + 1) * QUAD]

    def rows(x, i):
        return x[i * CHUNK:(i + 1) * CHUNK]

    gram = [_dot_nt(jnp.concatenate([blk(at_all, u), blk(rt_all, u)], axis=0),
                    jnp.concatenate([bd_of(blk(bt_all, u)), bd_of(blk(kt_all, u))], axis=0))
            for u in units]
    aab = [jnp.where(strict, g[0:CHUNK, 0:QUAD], 0.0) for g in gram]
    aak = [jnp.where(strict, g[0:CHUNK, QUAD:], 0.0) for g in gram]
    rb = [jnp.where(incl, g[CHUNK:, 0:QUAD], 0.0) for g in gram]
    rk = [jnp.where(incl, g[CHUNK:, QUAD:], 0.0) for g in gram]
    tinv = [eye_cat + a for a in aab]
    apow = [_dot(a, bd_of(a)) for a in aab]
    n = 2
    while 2 * n < CHUNK:
        sq = [_dot(jnp.concatenate([a, t], axis=0), bd_of(a)) for a, t in zip(apow, tinv)]
        apow = [rows(x, 0) for x in sq]
        tinv = [t + rows(x, 1) for t, x in zip(tinv, sq)]
        n *= 2
    tinv = [t + _dot(t, bd_of(a)) for a, t in zip(apow, tinv)]
    bht = [head_t(bd_of(blk(bh_all, u))) for u in units]
    kht = [head_t(bd_of(blk(kh_all, u))) for u in units]
    vres = [_dot(jnp.concatenate([a, m, kt], axis=0), bd_of(blk(v_all, u)))
            for a, m, kt, u in zip(aak, rk, kht, units)]
    wu = [_dot(t, jnp.concatenate([bd_of(blk(at_all, u)), bd_of(rows(x, 0))], axis=1))
          for t, x, u in zip(tinv, vres, units)]
    res = [_dot(jnp.concatenate([m, bt], axis=0),
                jnp.concatenate([bd_of(x[:, 0:QUAD]), bd_of(x[:, QUAD:])], axis=1))
           for m, bt, x in zip(rb, bht, wu)]
    rh = [blk(rt_all, u) + rows(x, 0)[:, 0:QUAD] for x, u in zip(res, units)]
    yh = [rows(x, 0)[:, QUAD:] + rows(y, 1) for x, y in zip(res, vres)]
    bw = [rows(x, 1)[:, 0:QUAD] for x in res]
    nt = [rows(x, 1)[:, QUAD:] + rows(y, 2) for x, y in zip(res, vres)]
    g_cat = [col_to_cat(jnp.sum(jnp.where(eye_bd, blk(g_c, u)[0:1, :], 0.0), axis=1, keepdims=True))
             for u in units]

    state = [p_scr[q] for q in range(nquad)]
    for i, (c, q) in enumerate(units):
        pp = state[q]
        x = _dot(jnp.concatenate([rh[i], bw[i]], axis=0), bd_of(pp))
        y_scr[c * CHUNK:(c + 1) * CHUNK, q * QUAD:(q + 1) * QUAD] = rows(x, 0) + yh[i]
        state[q] = g_cat[i] * pp + rows(x, 1) + nt[i]
    for q in range(nquad):
        p_scr[q] = state[q]

    bd = bd_ref[...]
    y = y_scr[...]
    d = y - _headsum(y, bd) * (1.0 / HEAD)
    var = _headsum1(d * d, bd) * (1.0 / HEAD)
    yn = d * lax.rsqrt(var + LNX_EPS)
    yn = yn * lnx_ref[0:1, :] + lnx_ref[1:2, :] + bon_ref[0]
    o_ref[0] = (yn * g_ref[0]).astype(BF16)


def _rwkv_call(r, k, v, kk, b, ld, bon, g, lnx, bd, *, tt):
    bsz, t, rw = r.shape
    tok = pl.BlockSpec((1, tt, rw), lambda bi, i: (bi, i, 0))
    full = lambda a: pl.BlockSpec(a.shape, lambda bi, i: (0,) * a.ndim)
    return pl.pallas_call(
        _rwkv_kernel,
        grid=(bsz, t // tt),
        in_specs=[tok] * 8 + [full(lnx), full(bd)],
        out_specs=tok,
        out_shape=jax.ShapeDtypeStruct((bsz, t, rw), BF16),
        scratch_shapes=[pltpu.VMEM((rw // QUAD, HEAD, QUAD), F32), pltpu.VMEM((tt, rw), F32)],
        compiler_params=pltpu.CompilerParams(dimension_semantics=("arbitrary", "arbitrary"),
                                             vmem_limit_bytes=VMEM_LIMIT),
        name="rwkv",
    )(r, k, v, kk, b, ld, bon, g, lnx, bd)


def _outmlp_kernel(x_ref, yc_ref, yr_ref, mod_ref, wo_ref, wup_ref, wdn_ref, fg_ref, o_ref, *, ff_block):
    cw = yc_ref.shape[2]
    y = (jnp.dot(yc_ref[0], wo_ref[0:cw, :], preferred_element_type=F32)
         + jnp.dot(yr_ref[0], wo_ref[cw:, :], preferred_element_type=F32))
    x1 = x_ref[0] + mod_ref[0, 2:3, :] * y
    h = (_rms(x1) * (1.0 + mod_ref[0, 4:5, :]) + mod_ref[0, 3:4, :]).astype(BF16)
    f = jnp.zeros_like(x1)
    for j in range(wup_ref.shape[1] // ff_block):
        hj = jnp.dot(h, wup_ref[:, j * ff_block:(j + 1) * ff_block], preferred_element_type=F32)
        hj = jnp.square(jnp.maximum(hj, 0.0)).astype(BF16)
        f = f + jnp.dot(hj, wdn_ref[j * ff_block:(j + 1) * ff_block, :], preferred_element_type=F32)
    x2 = x1 + mod_ref[0, 5:6, :] * f
    o_ref[0] = _rms(x2) * fg_ref[...]


def _outmlp_call(x, yc, yr, mod, wo, wup, wdn, fg, *, tt):
    bsz, t, d = x.shape
    tok = lambda w: pl.BlockSpec((1, tt, w), lambda b, i: (b, i, 0))
    res = lambda a: pl.BlockSpec(a.shape, lambda b, i: (0,) * a.ndim, pipeline_mode=pl.Buffered(1))
    return pl.pallas_call(
        functools.partial(_outmlp_kernel, ff_block=1024),
        grid=(bsz, t // tt),
        in_specs=[tok(d), tok(yc.shape[2]), tok(yr.shape[2]),
                  pl.BlockSpec((1,) + mod.shape[1:], lambda b, i: (b, 0, 0)),
                  res(wo), res(wup), res(wdn), res(fg)],
        out_specs=tok(d),
        out_shape=jax.ShapeDtypeStruct((bsz, t, d), F32),
        compiler_params=pltpu.CompilerParams(dimension_semantics=("arbitrary", "arbitrary"),
                                             vmem_limit_bytes=VMEM_LIMIT),
        name="outmlp",
    )(x, yc, yr, mod, wo, wup, wdn, fg)


def kernel(x, c, w_ada, b_ada, w_in, conv_w, rwkv_mu, w0, w2, a0, a2, g2, k_k, k_a, r_k,
           lnx_g, lnx_b, w_out, w_up, w_down, final_g):
    assert w_ada.shape[0] == 1, "single-layer block"
    bsz, t, d = x.shape
    rw = w0.shape[1]
    n_mod = w_ada.shape[2] // d

    mod = _mod_call(c, w_ada[0], b_ada).reshape(bsz, n_mod, d)

    zpad = jnp.zeros((PAIR - w2.shape[1], rw), F32)
    ww = jnp.concatenate([w2[0], zpad], axis=0).astype(BF16)
    wa = jnp.concatenate([zpad, a2[0]], axis=0).astype(BF16)
    pv = jnp.concatenate([w0, a0, k_k, k_a, r_k.reshape(1, rw), jnp.zeros((3, rw), F32)], axis=0)
    hid = jnp.arange(rw, dtype=jnp.int32) // HEAD
    bd = (hid[:, None] == hid[None, :]).astype(BF16)
    lnx = jnp.concatenate([lnx_g, lnx_b], axis=0)

    yc, r, k2, v, kk, b, ld, bon, g = _inproj_call(
        x, mod, w_in[0].astype(BF16), conv_w[0, :, 0, :], rwkv_mu, pv, ww, wa,
        g2[0].astype(BF16), bd, tt=256)
    yr = _rwkv_call(r, k2, v, kk, b, ld, bon, g, lnx, bd, tt=256)
    return _outmlp_call(x, yc, yr, mod, w_out[0].astype(BF16), w_up[0].astype(BF16),
                        w_down[0].astype(BF16), final_g.reshape(1, d), tt=512)
```

```python
import functools

import jax
import jax.numpy as jnp
from jax import lax
from jax.experimental import pallas as pl
from jax.experimental.pallas import tpu as pltpu

F32 = jnp.float32
BF16 = jnp.bfloat16

HEAD = 64
CHUNK = 64
PAIR = 2 * HEAD
QUAD = 4 * HEAD
EPS = 1e-6
LNX_EPS = 1e-5 * HEAD
NEG_EXP_M05 = -0.6065306597126334
VMEM_LIMIT = 56 * 1024 * 1024


def _dot(a, b):
    return jnp.dot(a.astype(BF16), b.astype(BF16), preferred_element_type=F32)


def _dot_nt(a, b):
    return lax.dot_general(a.astype(BF16), b.astype(BF16), (((1,), (1,)), ((), ())),
                           preferred_element_type=F32)


def _dot_f32(a, b):
    return jnp.dot(a, b, precision=lax.Precision.HIGHEST, preferred_element_type=F32)


def _split2(x):
    hi = x.astype(BF16)
    lo = (x - hi.astype(F32)).astype(BF16)
    return hi, lo


def _split3(x):
    hi = x.astype(BF16)
    r1 = x - hi.astype(F32)
    mid = r1.astype(BF16)
    lo = (r1 - mid.astype(F32)).astype(BF16)
    return hi, mid, lo


def _headsum(x, bd):
    hi, lo = _split2(x)
    return (jnp.dot(hi, bd, preferred_element_type=F32)
            + jnp.dot(lo, bd, preferred_element_type=F32))


def _headsum1(x, bd):
    return jnp.dot(x.astype(BF16), bd, preferred_element_type=F32)


def _rms(x):
    return x * lax.rsqrt(jnp.mean(x * x, axis=-1, keepdims=True) + EPS)


def _mod_kernel(c_ref, w_ref, b_ref, o_ref):
    c = c_ref[...]
    cond = c * jax.nn.sigmoid(c)
    o_ref[...] = _dot_f32(cond, w_ref[...]) + b_ref[...]


def _mod_call(c, w_ada, b_ada):
    bsz, d = c.shape
    n = w_ada.shape[1]
    bn = 1024
    return pl.pallas_call(
        _mod_kernel,
        grid=(n // bn,),
        in_specs=[pl.BlockSpec((bsz, d), lambda j: (0, 0)),
                  pl.BlockSpec((d, bn), lambda j: (0, j)),
                  pl.BlockSpec((1, bn), lambda j: (0, j))],
        out_specs=pl.BlockSpec((bsz, bn), lambda j: (0, j)),
        out_shape=jax.ShapeDtypeStruct((bsz, n), F32),
        compiler_params=pltpu.CompilerParams(dimension_semantics=("arbitrary",),
                                             vmem_limit_bytes=VMEM_LIMIT),
        name="mod",
    )(c, w_ada, b_ada)


SV_RH, SV_YH, SV_BW, SV_NT, SV_GC = range(5)


def _mix_kernel(x_ref, mod_ref, win_ref, cw_ref, mu_ref, pv_ref, ww_ref, wa_ref, g2_ref, bd_ref, lnx_ref,
                yc_ref, yr_ref,
                zc_ref, qc_ref, p_scr, y_scr, sv_ref, bg_ref, *, conv_w, rw, n_tiles):
    tt = x_ref.shape[1]
    step = pl.program_id(1)
    nquad = rw // QUAD
    nheads = QUAD // HEAD
    nchunk = tt // CHUNK

    @pl.when(step == 0)
    def _():
        zc_ref[...] = jnp.zeros_like(zc_ref)
        qc_ref[...] = jnp.zeros_like(qc_ref)
        p_scr[...] = jnp.zeros_like(p_scr)
        sv_ref[...] = jnp.zeros_like(sv_ref)
        bg_ref[...] = jnp.zeros_like(bg_ref)

    x = x_ref[0]
    h = _rms(x) * (1.0 + mod_ref[0, 1:2, :]) + mod_ref[0, 0:1, :]
    p = jnp.dot(h.astype(BF16), win_ref[...], preferred_element_type=F32)
    first = lax.broadcasted_iota(jnp.int32, (8, 1), 0) == 0

    def shift_rows(m, carry_row):
        rolled = pltpu.roll(m, 1, 0)
        return jnp.concatenate([jnp.where(first, carry_row, rolled[0:8]), rolled[8:]], axis=0)

    cw = conv_w
    z = p[:, cw:2 * cw] * p[:, 2 * cw:3 * cw]
    z1 = shift_rows(z, zc_ref[0:1, :])
    z2 = shift_rows(z1, zc_ref[1:2, :])
    zc_ref[0:1, :] = z[tt - 1:tt, :]
    zc_ref[1:2, :] = z1[tt - 1:tt, :]
    yc = (p[:, 0:cw] * (cw_ref[0:1, :] * z2 + cw_ref[1:2, :] * z1 + cw_ref[2:3, :] * z)).astype(BF16)

    q = p[:, 3 * cw:]
    qprev = shift_rows(q, qc_ref[0:1, :])
    qc_ref[0:1, :] = q[tt - 1:tt, :]
    q = q + (qprev - q) * mu_ref[...]
    r = q[:, 0:rw]
    k = q[:, rw:2 * rw]
    v_all = q[:, 2 * rw:3 * rw]
    wa = q[:, 3 * rw:3 * rw + PAIR]
    gl = q[:, 3 * rw + PAIR:]
    lane = lax.broadcasted_iota(jnp.int32, (1, PAIR), 1)
    wa = jnp.where(lane < HEAD, jnp.tanh(wa), wa)
    w_pre = _dot(wa, ww_ref[...])
    a_pre = _dot(wa, wa_ref[...])
    gate = _dot(jax.nn.sigmoid(gl), g2_ref[...])
    w0, a0, k_k, k_a, r_k = (pv_ref[i:i + 1, :] for i in range(5))
    ld = jax.nn.sigmoid(w0 + w_pre) * NEG_EXP_M05
    a = jax.nn.sigmoid(a0 + a_pre)
    bd = bd_ref[...]
    kk = k * k_k
    kk = kk / jnp.maximum(jnp.sqrt(_headsum1(kk * kk, bd)), 1e-12)
    k2 = k * (1.0 + (a - 1.0) * k_a)
    b = kk * a
    bonus = _headsum1(r * k2 * r_k, bd) * v_all

    ri = lax.broadcasted_iota(jnp.int32, (tt, tt), 0)
    ci = lax.broadcasted_iota(jnp.int32, (tt, tt), 1)
    tril = jnp.where(((ri // CHUNK) == (ci // CHUNK)) & (ri >= ci), 1.0, 0.0).astype(BF16)
    cum = sum(jnp.dot(tril, part, preferred_element_type=F32) for part in _split3(ld))
    tot = jnp.concatenate(
        [jnp.broadcast_to(cum[c * CHUNK + CHUNK - 1:(c + 1) * CHUNK, :], (CHUNK, rw))
         for c in range(nchunk)], axis=0)
    einv = jnp.exp(-cum)
    g_c = jnp.exp(tot)
    rt_all = r * jnp.exp(cum)
    at_all = -kk * jnp.exp(cum - ld)
    bt_all = b * einv
    kt_all = k2 * einv
    bh_all = bt_all * g_c
    kh_all = kt_all * g_c

    ti = lax.broadcasted_iota(jnp.int32, (CHUNK, QUAD), 0)
    li = lax.broadcasted_iota(jnp.int32, (CHUNK, QUAD), 1)
    strict = ti > (li % HEAD)
    incl = ti >= (li % HEAD)
    eye_cat = jnp.where(ti == (li % HEAD), 1.0, 0.0).astype(F32)
    eye_cat_b = eye_cat.astype(BF16)
    head_of_lane = li // HEAD
    qi = lax.broadcasted_iota(jnp.int32, (QUAD, QUAD), 0)
    qj = lax.broadcasted_iota(jnp.int32, (QUAD, QUAD), 1)
    eye_bd = qi == qj

    def bd_of(m):
        mb = m.astype(BF16)
        zero = jnp.zeros_like(mb)
        return jnp.concatenate([jnp.where(head_of_lane == hh, mb, zero) for hh in range(nheads)], axis=0)

    def head_t(mbd):
        return lax.dot_general(eye_cat_b, mbd, (((1,), (1,)), ((), ())), preferred_element_type=F32)

    def col_to_cat(col):
        out = col[0:HEAD]
        for hh in range(1, nheads):
            out = jnp.where(head_of_lane == hh, col[hh * HEAD:(hh + 1) * HEAD], out)
        return out

    units = [(c, qd) for c in range(nchunk) for qd in range(nquad)]

    def blk(m, u):
        c, qd = u
        return m[c * CHUNK:(c + 1) * CHUNK, qd * QUAD:(qd + 1) * QUAD]

    def rows(m, i):
        return m[i * CHUNK:(i + 1) * CHUNK]

    state = [p_scr[qd] for qd in range(nquad)]
    pending = list(range(nchunk))

    def chain_step():
        if not pending:
            return
        c = pending.pop(0)
        for qd in range(nquad):
            i = c * nquad + qd
            pp = state[qd]
            m = _dot(jnp.concatenate([sv_ref[SV_RH, i], sv_ref[SV_BW, i]], axis=0), bd_of(pp))
            y_scr[c * CHUNK:(c + 1) * CHUNK, qd * QUAD:(qd + 1) * QUAD] = rows(m, 0) + sv_ref[SV_YH, i]
            state[qd] = sv_ref[SV_GC, i] * pp + rows(m, 1) + sv_ref[SV_NT, i]

    gram = [_dot_nt(jnp.concatenate([blk(at_all, u), blk(rt_all, u)], axis=0),
                    jnp.concatenate([bd_of(blk(bt_all, u)), bd_of(blk(kt_all, u))], axis=0))
            for u in units]
    chain_step()
    aab = [jnp.where(strict, m[0:CHUNK, 0:QUAD], 0.0) for m in gram]
    aak = [jnp.where(strict, m[0:CHUNK, QUAD:], 0.0) for m in gram]
    rb = [jnp.where(incl, m[CHUNK:, 0:QUAD], 0.0) for m in gram]
    rk = [jnp.where(incl, m[CHUNK:, QUAD:], 0.0) for m in gram]
    tinv = [eye_cat + m for m in aab]
    apow = [_dot(m, bd_of(m)) for m in aab]
    n = 2
    while 2 * n < CHUNK:
        sq = [_dot(jnp.concatenate([m, t], axis=0), bd_of(m)) for m, t in zip(apow, tinv)]
        apow = [rows(m, 0) for m in sq]
        tinv = [t + rows(m, 1) for t, m in zip(tinv, sq)]
        n *= 2
        chain_step()
    tinv = [t + _dot(t, bd_of(m)) for m, t in zip(apow, tinv)]
    bht = [head_t(bd_of(blk(bh_all, u))) for u in units]
    kht = [head_t(bd_of(blk(kh_all, u))) for u in units]
    vres = [_dot(jnp.concatenate([m1, m2, m3], axis=0), bd_of(blk(v_all, u)))
            for m1, m2, m3, u in zip(aak, rk, kht, units)]
    while pending:
        chain_step()
    wu = [_dot(t, jnp.concatenate([bd_of(blk(at_all, u)), bd_of(rows(m, 0))], axis=1))
          for t, m, u in zip(tinv, vres, units)]

    for qd in range(nquad):
        p_scr[qd] = state[qd]
    y = y_scr[...]
    d = y - _headsum(y, bd) * (1.0 / HEAD)
    var = _headsum1(d * d, bd) * (1.0 / HEAD)
    yn = d * lax.rsqrt(var + LNX_EPS)
    yn = yn * lnx_ref[0:1, :] + lnx_ref[1:2, :] + bg_ref[0]
    yr_ref[0] = (yn * bg_ref[1]).astype(BF16)

    res = [_dot(jnp.concatenate([m, bt], axis=0),
                jnp.concatenate([bd_of(w[:, 0:QUAD]), bd_of(w[:, QUAD:])], axis=1))
           for m, bt, w in zip(rb, bht, wu)]
    for i, u in enumerate(units):
        sv_ref[SV_RH, i] = blk(rt_all, u) + rows(res[i], 0)[:, 0:QUAD]
        sv_ref[SV_YH, i] = rows(res[i], 0)[:, QUAD:] + rows(vres[i], 1)
        sv_ref[SV_BW, i] = rows(res[i], 1)[:, 0:QUAD]
        sv_ref[SV_NT, i] = rows(res[i], 1)[:, QUAD:] + rows(vres[i], 2)
        sv_ref[SV_GC, i] = col_to_cat(
            jnp.sum(jnp.where(eye_bd, blk(g_c, u)[0:1, :], 0.0), axis=1, keepdims=True))
    bg_ref[0] = bonus
    bg_ref[1] = gate

    @pl.when(step < n_tiles)
    def _():
        yc_ref[0] = yc


def _mix_call(x, mod, win, cw, mu, pv, ww, wa, g2, bd, lnx, *, tt):
    bsz, t, d = x.shape
    ptot = win.shape[1]
    conv_w = cw.shape[1]
    rw = pv.shape[1]
    n_tiles = t // tt
    full = lambda a: pl.BlockSpec(a.shape, lambda b, i: (0,) * a.ndim)
    cur = lambda w: pl.BlockSpec((1, tt, w), lambda b, i: (b, jnp.minimum(i, n_tiles - 1), 0))
    prev = lambda w: pl.BlockSpec((1, tt, w), lambda b, i: (b, jnp.maximum(i - 1, 0), 0))
    n_units = (tt // CHUNK) * (rw // QUAD)
    return pl.pallas_call(
        functools.partial(_mix_kernel, conv_w=conv_w, rw=rw, n_tiles=n_tiles),
        grid=(bsz, n_tiles + 1),
        in_specs=[cur(d), pl.BlockSpec((1,) + mod.shape[1:], lambda b, i: (b, 0, 0)),
                  full(win), full(cw), full(mu), full(pv), full(ww), full(wa), full(g2), full(bd),
                  full(lnx)],
        out_specs=[cur(conv_w), prev(rw)],
        out_shape=[jax.ShapeDtypeStruct((bsz, t, conv_w), BF16), jax.ShapeDtypeStruct((bsz, t, rw), BF16)],
        scratch_shapes=[pltpu.VMEM((8, conv_w), F32),
                        pltpu.VMEM((8, ptot - 3 * conv_w), F32),
                        pltpu.VMEM((rw // QUAD, HEAD, QUAD), F32),
                        pltpu.VMEM((tt, rw), F32),
                        pltpu.VMEM((5, n_units, CHUNK, QUAD), F32),
                        pltpu.VMEM((2, tt, rw), F32)],
        compiler_params=pltpu.CompilerParams(dimension_semantics=("arbitrary", "arbitrary"),
                                             vmem_limit_bytes=VMEM_LIMIT),
        name="mix",
    )(x, mod, win, cw, mu, pv, ww, wa, g2, bd, lnx)


def _outmlp_kernel(x_ref, yc_ref, yr_ref, mod_ref, wo_ref, wup_ref, wdn_ref, fg_ref, o_ref, *, ff_block):
    cw = yc_ref.shape[2]
    y = (jnp.dot(yc_ref[0], wo_ref[0:cw, :], preferred_element_type=F32)
         + jnp.dot(yr_ref[0], wo_ref[cw:, :], preferred_element_type=F32))
    x1 = x_ref[0] + mod_ref[0, 2:3, :] * y
    h = (_rms(x1) * (1.0 + mod_ref[0, 4:5, :]) + mod_ref[0, 3:4, :]).astype(BF16)
    f = jnp.zeros_like(x1)
    for j in range(wup_ref.shape[1] // ff_block):
        hj = jnp.dot(h, wup_ref[:, j * ff_block:(j + 1) * ff_block], preferred_element_type=F32)
        hj = jnp.square(jnp.maximum(hj, 0.0)).astype(BF16)
        f = f + jnp.dot(hj, wdn_ref[j * ff_block:(j + 1) * ff_block, :], preferred_element_type=F32)
    x2 = x1 + mod_ref[0, 5:6, :] * f
    o_ref[0] = _rms(x2) * fg_ref[...]


def _outmlp_call(x, yc, yr, mod, wo, wup, wdn, fg, *, tt):
    bsz, t, d = x.shape
    tok = lambda w: pl.BlockSpec((1, tt, w), lambda b, i: (b, i, 0))
    res = lambda a: pl.BlockSpec(a.shape, lambda b, i: (0,) * a.ndim, pipeline_mode=pl.Buffered(1))
    return pl.pallas_call(
        functools.partial(_outmlp_kernel, ff_block=1024),
        grid=(bsz, t // tt),
        in_specs=[tok(d), tok(yc.shape[2]), tok(yr.shape[2]),
                  pl.BlockSpec((1,) + mod.shape[1:], lambda b, i: (b, 0, 0)),
                  res(wo), res(wup), res(wdn), res(fg)],
        out_specs=tok(d),
        out_shape=jax.ShapeDtypeStruct((bsz, t, d), F32),
        compiler_params=pltpu.CompilerParams(dimension_semantics=("arbitrary", "arbitrary"),
                                             vmem_limit_bytes=VMEM_LIMIT),
        name="outmlp",
    )(x, yc, yr, mod, wo, wup, wdn, fg)


def kernel(x, c, w_ada, b_ada, w_in, conv_w, rwkv_mu, w0, w2, a0, a2, g2, k_k, k_a, r_k,
           lnx_g, lnx_b, w_out, w_up, w_down, final_g):
    assert w_ada.shape[0] == 1, "single-layer block"
    bsz, t, d = x.shape
    rw = w0.shape[1]
    n_mod = w_ada.shape[2] // d

    mod = _mod_call(c, w_ada[0], b_ada).reshape(bsz, n_mod, d)

    zpad = jnp.zeros((PAIR - w2.shape[1], rw), F32)
    ww = jnp.concatenate([w2[0], zpad], axis=0).astype(BF16)
    wa = jnp.concatenate([zpad, a2[0]], axis=0).astype(BF16)
    pv = jnp.concatenate([w0, a0, k_k, k_a, r_k.reshape(1, rw), jnp.zeros((3, rw), F32)], axis=0)
    hid = jnp.arange(rw, dtype=jnp.int32) // HEAD
    bd = (hid[:, None] == hid[None, :]).astype(BF16)
    lnx = jnp.concatenate([lnx_g, lnx_b], axis=0)

    yc, yr = _mix_call(x, mod, w_in[0].astype(BF16), conv_w[0, :, 0, :], rwkv_mu, pv, ww, wa,
                       g2[0].astype(BF16), bd, lnx, tt=256)
    return _outmlp_call(x, yc, yr, mod, w_out[0].astype(BF16), w_up[0].astype(BF16),
                        w_down[0].astype(BF16), final_g.reshape(1, d), tt=512)
```

```python
import functools

import jax
import jax.numpy as jnp
from jax import lax
from jax.experimental import pallas as pl
from jax.experimental.pallas import tpu as pltpu

F32 = jnp.float32
BF16 = jnp.bfloat16

HEAD = 64
CHUNK = 64
PAIR = 2 * HEAD
QUAD = 4 * HEAD
EPS = 1e-6
LNX_EPS = 1e-5 * HEAD
NEG_EXP_M05 = -0.6065306597126334
VMEM_LIMIT = 56 * 1024 * 1024


def _dot(a, b):
    return jnp.dot(a.astype(BF16), b.astype(BF16), preferred_element_type=F32)


def _dot_nt(a, b):
    return lax.dot_general(a.astype(BF16), b.astype(BF16), (((1,), (1,)), ((), ())),
                           preferred_element_type=F32)


def _dot_f32(a, b):
    return jnp.dot(a, b, precision=lax.Precision.HIGHEST, preferred_element_type=F32)


def _split2(x):
    hi = x.astype(BF16)
    lo = (x - hi.astype(F32)).astype(BF16)
    return hi, lo


def _split3(x):
    hi = x.astype(BF16)
    r1 = x - hi.astype(F32)
    mid = r1.astype(BF16)
    lo = (r1 - mid.astype(F32)).astype(BF16)
    return hi, mid, lo


def _quads_to_rows(x):
    return jnp.concatenate([x[:, i:i + QUAD] for i in range(0, x.shape[1], QUAD)], axis=0)


def _rows_to_quads(x, t):
    return jnp.concatenate([x[i:i + t] for i in range(0, x.shape[0], t)], axis=1)


def _headsum1(x, bd):
    s = jnp.dot(_quads_to_rows(x.astype(BF16)), bd, preferred_element_type=F32)
    return _rows_to_quads(s, x.shape[0])


def _headsum(x, bd):
    hi, lo = _split2(x)
    s = jnp.dot(_quads_to_rows(jnp.concatenate([hi, lo], axis=0)), bd, preferred_element_type=F32)
    s = _rows_to_quads(s, 2 * x.shape[0])
    return s[0:x.shape[0]] + s[x.shape[0]:]


def _rms(x):
    return x * lax.rsqrt(jnp.mean(x * x, axis=-1, keepdims=True) + EPS)


def _mod_kernel(c_ref, w_ref, b_ref, o_ref):
    c = c_ref[...]
    cond = c * jax.nn.sigmoid(c)
    o_ref[...] = _dot_f32(cond, w_ref[...]) + b_ref[...]


def _mod_call(c, w_ada, b_ada):
    bsz, d = c.shape
    n = w_ada.shape[1]
    bn = 1024
    return pl.pallas_call(
        _mod_kernel,
        grid=(n // bn,),
        in_specs=[pl.BlockSpec((bsz, d), lambda j: (0, 0)),
                  pl.BlockSpec((d, bn), lambda j: (0, j)),
                  pl.BlockSpec((1, bn), lambda j: (0, j))],
        out_specs=pl.BlockSpec((bsz, bn), lambda j: (0, j)),
        out_shape=jax.ShapeDtypeStruct((bsz, n), F32),
        compiler_params=pltpu.CompilerParams(dimension_semantics=("arbitrary",),
                                             vmem_limit_bytes=VMEM_LIMIT),
        name="mod",
    )(c, w_ada, b_ada)


SV_RH, SV_YH, SV_BW, SV_NT, SV_GC = range(5)
AB_AT, AB_BT, AB_KT, AB_V = range(4)


def _mix_kernel(x_ref, mod_ref, win_ref, cw_ref, mu_ref, pv_ref, ww_ref, wa_ref, g2_ref, bd_ref, lnx_ref,
                yc_ref, yr_ref,
                zc_ref, qc_ref, p_scr, y_scr, sv_ref, bg_ref, ab_ref, rt_ref, gc_ref,
                *, conv_w, rw, n_tiles):
    tt = x_ref.shape[1]
    step = pl.program_id(1)
    nquad = rw // QUAD
    nheads = QUAD // HEAD
    nchunk = tt // CHUNK

    @pl.when(step == 0)
    def _():
        for ref in (zc_ref, qc_ref, p_scr, sv_ref, bg_ref, ab_ref, rt_ref, gc_ref):
            ref[...] = jnp.zeros_like(ref)

    bd = bd_ref[...]
    ti = lax.broadcasted_iota(jnp.int32, (CHUNK, QUAD), 0)
    li = lax.broadcasted_iota(jnp.int32, (CHUNK, QUAD), 1)
    strict = ti > (li % HEAD)
    incl = ti >= (li % HEAD)
    eye_cat = jnp.where(ti == (li % HEAD), 1.0, 0.0).astype(F32)
    eye_cat_b = eye_cat.astype(BF16)
    head_of_lane = li // HEAD
    qi = lax.broadcasted_iota(jnp.int32, (QUAD, QUAD), 0)
    qj = lax.broadcasted_iota(jnp.int32, (QUAD, QUAD), 1)
    eye_bd = qi == qj

    def bd_of(m):
        mb = m.astype(BF16)
        zero = jnp.zeros_like(mb)
        return jnp.concatenate([jnp.where(head_of_lane == hh, mb, zero) for hh in range(nheads)], axis=0)

    def col_to_cat(col):
        out = col[0:HEAD]
        for hh in range(1, nheads):
            out = jnp.where(head_of_lane == hh, col[hh * HEAD:(hh + 1) * HEAD], out)
        return out

    units = [(c, qd) for c in range(nchunk) for qd in range(nquad)]

    def rows(m, i):
        return m[i * CHUNK:(i + 1) * CHUNK]

    def ab(kind, u):
        c, qd = u
        return ab_ref[kind, c * CHUNK:(c + 1) * CHUNK, qd * QUAD:(qd + 1) * QUAD]

    def rt_prev(u):
        c, qd = u
        return rt_ref[c * CHUNK:(c + 1) * CHUNK, qd * QUAD:(qd + 1) * QUAD]

    state = [p_scr[qd] for qd in range(nquad)]
    pending = list(range(nchunk))

    def chain_step():
        if not pending:
            return
        c = pending.pop(0)
        for qd in range(nquad):
            i = c * nquad + qd
            pp = state[qd]
            m = _dot(jnp.concatenate([sv_ref[SV_RH, i], sv_ref[SV_BW, i]], axis=0), bd_of(pp))
            y_scr[c * CHUNK:(c + 1) * CHUNK, qd * QUAD:(qd + 1) * QUAD] = rows(m, 0) + sv_ref[SV_YH, i]
            state[qd] = sv_ref[SV_GC, i] * pp + rows(m, 1) + sv_ref[SV_NT, i]

    x = x_ref[0]
    h = _rms(x) * (1.0 + mod_ref[0, 1:2, :]) + mod_ref[0, 0:1, :]
    p = jnp.dot(h.astype(BF16), win_ref[...], preferred_element_type=F32)
    first = lax.broadcasted_iota(jnp.int32, (8, 1), 0) == 0
    cw = conv_w

    def shift_rows(m, carry_row):
        rolled = pltpu.roll(m, 1, 0)
        return jnp.concatenate([jnp.where(first, carry_row, rolled[0:8]), rolled[8:]], axis=0)

    gram = [_dot_nt(jnp.concatenate([ab(AB_AT, u), rt_prev(u).astype(BF16), eye_cat_b], axis=0),
                    jnp.concatenate([bd_of(ab(AB_BT, u)), bd_of(ab(AB_KT, u))], axis=0))
            for u in units]
    chain_step()

    z = p[:, cw:2 * cw] * p[:, 2 * cw:3 * cw]
    z1 = shift_rows(z, zc_ref[0:1, :])
    z2 = shift_rows(z1, zc_ref[1:2, :])
    zc_ref[0:1, :] = z[tt - 1:tt, :]
    zc_ref[1:2, :] = z1[tt - 1:tt, :]
    yc = (p[:, 0:cw] * (cw_ref[0:1, :] * z2 + cw_ref[1:2, :] * z1 + cw_ref[2:3, :] * z)).astype(BF16)

    aab = [jnp.where(strict, rows(m, 0)[:, 0:QUAD], 0.0) for m in gram]
    aak = [jnp.where(strict, rows(m, 0)[:, QUAD:], 0.0) for m in gram]
    rb = [jnp.where(incl, rows(m, 1)[:, 0:QUAD], 0.0) for m in gram]
    rk = [jnp.where(incl, rows(m, 1)[:, QUAD:], 0.0) for m in gram]
    btt = [rows(m, 2)[:, 0:QUAD] for m in gram]
    ktt = [rows(m, 2)[:, QUAD:] for m in gram]
    tinv = [eye_cat + m for m in aab]
    apow = [_dot(m, bd_of(m)) for m in aab]

    q = p[:, 3 * cw:]
    qprev = shift_rows(q, qc_ref[0:1, :])
    qc_ref[0:1, :] = q[tt - 1:tt, :]
    q = q + (qprev - q) * mu_ref[...]
    r = q[:, 0:rw]
    k = q[:, rw:2 * rw]
    v_new = q[:, 2 * rw:3 * rw]
    wa = q[:, 3 * rw:3 * rw + PAIR]
    gl = q[:, 3 * rw + PAIR:]
    lane = lax.broadcasted_iota(jnp.int32, (1, PAIR), 1)
    wa = jnp.where(lane < HEAD, jnp.tanh(wa), wa)
    w_pre = _dot(wa, ww_ref[...])
    a_pre = _dot(wa, wa_ref[...])
    gate = _dot(jax.nn.sigmoid(gl), g2_ref[...])
    w0, a0, k_k, k_a, r_k = (pv_ref[i:i + 1, :] for i in range(5))

    def level(apow, tinv):
        sq = [_dot(jnp.concatenate([m, t], axis=0), bd_of(m)) for m, t in zip(apow, tinv)]
        return [rows(m, 0) for m in sq], [t + rows(m, 1) for t, m in zip(tinv, sq)]

    n_levels = 0
    while (4 << n_levels) < CHUNK:
        n_levels += 1
    for _ in range(n_levels // 2):
        apow, tinv = level(apow, tinv)
        chain_step()

    ld = jax.nn.sigmoid(w0 + w_pre) * NEG_EXP_M05
    a = jax.nn.sigmoid(a0 + a_pre)
    kk = k * k_k
    kk = kk / jnp.maximum(jnp.sqrt(_headsum1(kk * kk, bd)), 1e-12)
    k2 = k * (1.0 + (a - 1.0) * k_a)
    b = kk * a
    bonus = _headsum1(r * k2 * r_k, bd) * v_new

    for _ in range(n_levels - n_levels // 2):
        apow, tinv = level(apow, tinv)
        chain_step()
    tinv = [t + _dot(t, bd_of(m)) for m, t in zip(apow, tinv)]

    ri = lax.broadcasted_iota(jnp.int32, (tt, tt), 0)
    ci = lax.broadcasted_iota(jnp.int32, (tt, tt), 1)
    tril = jnp.where(((ri // CHUNK) == (ci // CHUNK)) & (ri >= ci), 1.0, 0.0).astype(BF16)
    cum = sum(jnp.dot(tril, part, preferred_element_type=F32) for part in _split3(ld))
    last = [cum[c * CHUNK + CHUNK - 1:(c + 1) * CHUNK, :] for c in range(nchunk)]
    einv = jnp.exp(-cum)

    vres = [_dot(jnp.concatenate([m1, m2, m3], axis=0), bd_of(ab(AB_V, u)))
            for m1, m2, m3, u in zip(aak, rk, ktt, units)]
    while pending:
        chain_step()

    rt_new = r * jnp.exp(cum)
    at_new = -kk * jnp.exp(cum - ld)
    bt_new = b * einv
    kt_new = k2 * einv

    wu = [_dot(t, jnp.concatenate([bd_of(ab(AB_AT, u)), bd_of(rows(m, 0))], axis=1))
          for t, m, u in zip(tinv, vres, units)]

    for qd in range(nquad):
        p_scr[qd] = state[qd]
    slot_out = lax.rem(step + 1, jnp.int32(3))
    y = y_scr[...]
    d = y - _headsum(y, bd) * (1.0 / HEAD)
    var = _headsum1(d * d, bd) * (1.0 / HEAD)
    yn = d * lax.rsqrt(var + LNX_EPS)
    yn = yn * lnx_ref[0:1, :] + lnx_ref[1:2, :] + bg_ref[slot_out, 0]
    yr_ref[0] = (yn * bg_ref[slot_out, 1]).astype(BF16)

    res = [_dot(jnp.concatenate([m, bt], axis=0),
                jnp.concatenate([bd_of(w[:, 0:QUAD]), bd_of(w[:, QUAD:])], axis=1))
           for m, bt, w in zip(rb, btt, wu)]
    for i, u in enumerate(units):
        c, qd = u
        g_row = gc_ref[c:c + 1, qd * QUAD:(qd + 1) * QUAD]
        g_cat = col_to_cat(jnp.sum(jnp.where(eye_bd, g_row, 0.0), axis=1, keepdims=True))
        sv_ref[SV_RH, i] = rt_prev(u) + rows(res[i], 0)[:, 0:QUAD]
        sv_ref[SV_YH, i] = rows(res[i], 0)[:, QUAD:] + rows(vres[i], 1)
        sv_ref[SV_BW, i] = g_cat * rows(res[i], 1)[:, 0:QUAD]
        sv_ref[SV_NT, i] = g_cat * (rows(res[i], 1)[:, QUAD:] + rows(vres[i], 2))
        sv_ref[SV_GC, i] = g_cat

    ab_ref[AB_AT] = at_new.astype(BF16)
    ab_ref[AB_BT] = bt_new.astype(BF16)
    ab_ref[AB_KT] = kt_new.astype(BF16)
    ab_ref[AB_V] = v_new.astype(BF16)
    rt_ref[...] = rt_new
    for c in range(nchunk):
        gc_ref[c:c + 1, :] = jnp.exp(last[c])
    slot_in = lax.rem(step, jnp.int32(3))
    bg_ref[slot_in, 0] = bonus
    bg_ref[slot_in, 1] = gate

    @pl.when(step < n_tiles)
    def _():
        yc_ref[0] = yc


def _mix_call(x, mod, win, cw, mu, pv, ww, wa, g2, bd, lnx, *, tt):
    bsz, t, d = x.shape
    ptot = win.shape[1]
    conv_w = cw.shape[1]
    rw = pv.shape[1]
    n_tiles = t // tt
    assert tt // CHUNK <= 8, "gc scratch holds one row per chunk"
    full = lambda a: pl.BlockSpec(a.shape, lambda b, i: (0,) * a.ndim)
    cur = lambda w: pl.BlockSpec((1, tt, w), lambda b, i: (b, jnp.minimum(i, n_tiles - 1), 0))
    lag2 = lambda w: pl.BlockSpec((1, tt, w), lambda b, i: (b, jnp.clip(i - 2, 0, n_tiles - 1), 0))
    n_units = (tt // CHUNK) * (rw // QUAD)
    return pl.pallas_call(
        functools.partial(_mix_kernel, conv_w=conv_w, rw=rw, n_tiles=n_tiles),
        grid=(bsz, n_tiles + 2),
        in_specs=[cur(d), pl.BlockSpec((1,) + mod.shape[1:], lambda b, i: (b, 0, 0)),
                  full(win), full(cw), full(mu), full(pv), full(ww), full(wa), full(g2), full(bd),
                  full(lnx)],
        out_specs=[cur(conv_w), lag2(rw)],
        out_shape=[jax.ShapeDtypeStruct((bsz, t, conv_w), BF16), jax.ShapeDtypeStruct((bsz, t, rw), BF16)],
        scratch_shapes=[pltpu.VMEM((8, conv_w), F32),
                        pltpu.VMEM((8, ptot - 3 * conv_w), F32),
                        pltpu.VMEM((rw // QUAD, HEAD, QUAD), F32),
                        pltpu.VMEM((tt, rw), F32),
                        pltpu.VMEM((5, n_units, CHUNK, QUAD), F32),
                        pltpu.VMEM((3, 2, tt, rw), F32),
                        pltpu.VMEM((4, tt, rw), BF16),
                        pltpu.VMEM((tt, rw), F32),
                        pltpu.VMEM((8, rw), F32)],
        compiler_params=pltpu.CompilerParams(dimension_semantics=("arbitrary", "arbitrary"),
                                             vmem_limit_bytes=VMEM_LIMIT),
        name="mix",
    )(x, mod, win, cw, mu, pv, ww, wa, g2, bd, lnx)


def _outmlp_kernel(x_ref, yc_ref, yr_ref, mod_ref, wo_ref, wup_ref, wdn_ref, fg_ref, o_ref, *, ff_block):
    cw = yc_ref.shape[2]
    y = (jnp.dot(yc_ref[0], wo_ref[0:cw, :], preferred_element_type=F32)
         + jnp.dot(yr_ref[0], wo_ref[cw:, :], preferred_element_type=F32))
    x1 = x_ref[0] + mod_ref[0, 2:3, :] * y
    h = (_rms(x1) * (1.0 + mod_ref[0, 4:5, :]) + mod_ref[0, 3:4, :]).astype(BF16)
    f = jnp.zeros_like(x1)
    for j in range(wup_ref.shape[1] // ff_block):
        hj = jnp.dot(h, wup_ref[:, j * ff_block:(j + 1) * ff_block], preferred_element_type=F32)
        hj = jnp.square(jnp.maximum(hj, 0.0)).astype(BF16)
        f = f + jnp.dot(hj, wdn_ref[j * ff_block:(j + 1) * ff_block, :], preferred_element_type=F32)
    x2 = x1 + mod_ref[0, 5:6, :] * f
    o_ref[0] = _rms(x2) * fg_ref[...]


def _outmlp_call(x, yc, yr, mod, wo, wup, wdn, fg, *, tt):
    bsz, t, d = x.shape
    tok = lambda w: pl.BlockSpec((1, tt, w), lambda b, i: (b, i, 0))
    res = lambda a: pl.BlockSpec(a.shape, lambda b, i: (0,) * a.ndim, pipeline_mode=pl.Buffered(1))
    return pl.pallas_call(
        functools.partial(_outmlp_kernel, ff_block=1024),
        grid=(bsz, t // tt),
        in_specs=[tok(d), tok(yc.shape[2]), tok(yr.shape[2]),
                  pl.BlockSpec((1,) + mod.shape[1:], lambda b, i: (b, 0, 0)),
                  res(wo), res(wup), res(wdn), res(fg)],
        out_specs=tok(d),
        out_shape=jax.ShapeDtypeStruct((bsz, t, d), F32),
        compiler_params=pltpu.CompilerParams(dimension_semantics=("arbitrary", "arbitrary"),
                                             vmem_limit_bytes=VMEM_LIMIT),
        name="outmlp",
    )(x, yc, yr, mod, wo, wup, wdn, fg)


def kernel(x, c, w_ada, b_ada, w_in, conv_w, rwkv_mu, w0, w2, a0, a2, g2, k_k, k_a, r_k,
           lnx_g, lnx_b, w_out, w_up, w_down, final_g):
    assert w_ada.shape[0] == 1, "single-layer block"
    bsz, t, d = x.shape
    rw = w0.shape[1]
    n_mod = w_ada.shape[2] // d

    mod = _mod_call(c, w_ada[0], b_ada).reshape(bsz, n_mod, d)

    zpad = jnp.zeros((PAIR - w2.shape[1], rw), F32)
    ww = jnp.concatenate([w2[0], zpad], axis=0).astype(BF16)
    wa = jnp.concatenate([zpad, a2[0]], axis=0).astype(BF16)
    pv = jnp.concatenate([w0, a0, k_k, k_a, r_k.reshape(1, rw), jnp.zeros((3, rw), F32)], axis=0)
    hid = jnp.arange(QUAD, dtype=jnp.int32) // HEAD
    bd = (hid[:, None] == hid[None, :]).astype(BF16)
    lnx = jnp.concatenate([lnx_g, lnx_b], axis=0)

    yc, yr = _mix_call(x, mod, w_in[0].astype(BF16), conv_w[0, :, 0, :], rwkv_mu, pv, ww, wa,
                       g2[0].astype(BF16), bd, lnx, tt=256)
    return _outmlp_call(x, yc, yr, mod, w_out[0].astype(BF16), w_up[0].astype(BF16),
                        w_down[0].astype(BF16), final_g.reshape(1, d), tt=512)
```

```python
import functools

import jax
import jax.numpy as jnp
from jax import lax
from jax.experimental import pallas as pl
from jax.experimental.pallas import tpu as pltpu

F32 = jnp.float32
BF16 = jnp.bfloat16

HEAD = 64
CHUNK = 64
PAIR = 2 * HEAD
QUAD = 4 * HEAD
EPS = 1e-6
LNX_EPS = 1e-5 * HEAD
NEG_EXP_M05 = -0.6065306597126334
VMEM_LIMIT = 56 * 1024 * 1024


def _dot(a, b):
    return jnp.dot(a.astype(BF16), b.astype(BF16), preferred_element_type=F32)


def _dot_nt(a, b):
    return lax.dot_general(a.astype(BF16), b.astype(BF16), (((1,), (1,)), ((), ())),
                           preferred_element_type=F32)


def _dot_f32(a, b):
    return jnp.dot(a, b, precision=lax.Precision.HIGHEST, preferred_element_type=F32)


def _split2(x):
    hi = x.astype(BF16)
    lo = (x - hi.astype(F32)).astype(BF16)
    return hi, lo


def _split3(x):
    hi = x.astype(BF16)
    r1 = x - hi.astype(F32)
    mid = r1.astype(BF16)
    lo = (r1 - mid.astype(F32)).astype(BF16)
    return hi, mid, lo


def _quads_to_rows(x):
    return jnp.concatenate([x[:, i:i + QUAD] for i in range(0, x.shape[1], QUAD)], axis=0)


def _rows_to_quads(x, t):
    return jnp.concatenate([x[i:i + t] for i in range(0, x.shape[0], t)], axis=1)


def _headsum1(x, bd):
    s = jnp.dot(_quads_to_rows(x.astype(BF16)), bd, preferred_element_type=F32)
    return _rows_to_quads(s, x.shape[0])


def _headsum(x, bd):
    hi, lo = _split2(x)
    s = jnp.dot(_quads_to_rows(jnp.concatenate([hi, lo], axis=0)), bd, preferred_element_type=F32)
    s = _rows_to_quads(s, 2 * x.shape[0])
    return s[0:x.shape[0]] + s[x.shape[0]:]


def _rms(x):
    return x * lax.rsqrt(jnp.mean(x * x, axis=-1, keepdims=True) + EPS)


def _mod_kernel(c_ref, w_ref, b_ref, o_ref):
    c = c_ref[...]
    cond = c * jax.nn.sigmoid(c)
    o_ref[...] = _dot_f32(cond, w_ref[...]) + b_ref[...]


def _mod_call(c, w_ada, b_ada):
    bsz, d = c.shape
    n = w_ada.shape[1]
    bn = 1024
    return pl.pallas_call(
        _mod_kernel,
        grid=(n // bn,),
        in_specs=[pl.BlockSpec((bsz, d), lambda j: (0, 0)),
                  pl.BlockSpec((d, bn), lambda j: (0, j)),
                  pl.BlockSpec((1, bn), lambda j: (0, j))],
        out_specs=pl.BlockSpec((bsz, bn), lambda j: (0, j)),
        out_shape=jax.ShapeDtypeStruct((bsz, n), F32),
        compiler_params=pltpu.CompilerParams(dimension_semantics=("arbitrary",),
                                             vmem_limit_bytes=VMEM_LIMIT),
        name="mod",
    )(c, w_ada, b_ada)


SV_RH, SV_YH, SV_BW, SV_NT, SV_GC = range(5)
AB_AT, AB_BT, AB_KT, AB_V = range(4)


def _mix_kernel(x_ref, mod_ref, win_ref, cw_ref, mu_ref, pv_ref, ww_ref, wa_ref, g2_ref, bd_ref, lnx_ref,
                yc_ref, yr_ref,
                zc_ref, qc_ref, p_scr, y_scr, sv_ref, bg_ref, ab_ref, rt_ref, gc_ref,
                *, conv_w, rw, n_tiles, tiles_per_seq):
    tt = x_ref.shape[1]
    step = pl.program_id(0)
    seq_start = lax.rem(step, jnp.int32(tiles_per_seq)) == 0
    chain_seq_start = lax.rem(step - 2, jnp.int32(tiles_per_seq)) == 0
    nquad = rw // QUAD
    nheads = QUAD // HEAD
    nchunk = tt // CHUNK

    @pl.when(step == 0)
    def _():
        for ref in (zc_ref, qc_ref, p_scr, sv_ref, bg_ref, ab_ref, rt_ref, gc_ref):
            ref[...] = jnp.zeros_like(ref)

    bd = bd_ref[...]
    ti = lax.broadcasted_iota(jnp.int32, (CHUNK, QUAD), 0)
    li = lax.broadcasted_iota(jnp.int32, (CHUNK, QUAD), 1)
    strict = ti > (li % HEAD)
    incl = ti >= (li % HEAD)
    eye_cat = jnp.where(ti == (li % HEAD), 1.0, 0.0).astype(F32)
    eye_cat_b = eye_cat.astype(BF16)
    head_of_lane = li // HEAD
    qi = lax.broadcasted_iota(jnp.int32, (QUAD, QUAD), 0)
    qj = lax.broadcasted_iota(jnp.int32, (QUAD, QUAD), 1)
    eye_bd = qi == qj

    def bd_of(m):
        mb = m.astype(BF16)
        zero = jnp.zeros_like(mb)
        return jnp.concatenate([jnp.where(head_of_lane == hh, mb, zero) for hh in range(nheads)], axis=0)

    def col_to_cat(col):
        out = col[0:HEAD]
        for hh in range(1, nheads):
            out = jnp.where(head_of_lane == hh, col[hh * HEAD:(hh + 1) * HEAD], out)
        return out

    units = [(c, qd) for c in range(nchunk) for qd in range(nquad)]

    def rows(m, i):
        return m[i * CHUNK:(i + 1) * CHUNK]

    def ab(kind, u):
        c, qd = u
        return ab_ref[kind, c * CHUNK:(c + 1) * CHUNK, qd * QUAD:(qd + 1) * QUAD]

    def rt_prev(u):
        c, qd = u
        return rt_ref[c * CHUNK:(c + 1) * CHUNK, qd * QUAD:(qd + 1) * QUAD]

    state = [jnp.where(chain_seq_start, 0.0, p_scr[qd]) for qd in range(nquad)]
    pending = list(range(nchunk))

    def chain_step():
        if not pending:
            return
        c = pending.pop(0)
        for qd in range(nquad):
            i = c * nquad + qd
            pp = state[qd]
            m = _dot(jnp.concatenate([sv_ref[SV_RH, i], sv_ref[SV_BW, i]], axis=0), bd_of(pp))
            y_scr[c * CHUNK:(c + 1) * CHUNK, qd * QUAD:(qd + 1) * QUAD] = rows(m, 0) + sv_ref[SV_YH, i]
            state[qd] = sv_ref[SV_GC, i] * pp + rows(m, 1) + sv_ref[SV_NT, i]

    x = x_ref[0]
    h = _rms(x) * (1.0 + mod_ref[0, 1:2, :]) + mod_ref[0, 0:1, :]
    p = jnp.dot(h.astype(BF16), win_ref[...], preferred_element_type=F32)
    first = lax.broadcasted_iota(jnp.int32, (8, 1), 0) == 0
    cw = conv_w

    def shift_rows(m, carry_row):
        rolled = pltpu.roll(m, 1, 0)
        return jnp.concatenate([jnp.where(first, carry_row, rolled[0:8]), rolled[8:]], axis=0)

    gram = [_dot_nt(jnp.concatenate([ab(AB_AT, u), rt_prev(u).astype(BF16), eye_cat_b], axis=0),
                    jnp.concatenate([bd_of(ab(AB_BT, u)), bd_of(ab(AB_KT, u))], axis=0))
            for u in units]
    chain_step()

    z = p[:, cw:2 * cw] * p[:, 2 * cw:3 * cw]
    z1 = shift_rows(z, jnp.where(seq_start, 0.0, zc_ref[0:1, :]))
    z2 = shift_rows(z1, jnp.where(seq_start, 0.0, zc_ref[1:2, :]))
    zc_ref[0:1, :] = z[tt - 1:tt, :]
    zc_ref[1:2, :] = z1[tt - 1:tt, :]
    yc = (p[:, 0:cw] * (cw_ref[0:1, :] * z2 + cw_ref[1:2, :] * z1 + cw_ref[2:3, :] * z)).astype(BF16)

    aab = [jnp.where(strict, rows(m, 0)[:, 0:QUAD], 0.0) for m in gram]
    aak = [jnp.where(strict, rows(m, 0)[:, QUAD:], 0.0) for m in gram]
    rb = [jnp.where(incl, rows(m, 1)[:, 0:QUAD], 0.0) for m in gram]
    rk = [jnp.where(incl, rows(m, 1)[:, QUAD:], 0.0) for m in gram]
    btt = [rows(m, 2)[:, 0:QUAD] for m in gram]
    ktt = [rows(m, 2)[:, QUAD:] for m in gram]
    tinv = [eye_cat + m for m in aab]
    apow = [_dot(m, bd_of(m)) for m in aab]

    q = p[:, 3 * cw:]
    qprev = shift_rows(q, jnp.where(seq_start, 0.0, qc_ref[0:1, :]))
    qc_ref[0:1, :] = q[tt - 1:tt, :]
    q = q + (qprev - q) * mu_ref[...]
    r = q[:, 0:rw]
    k = q[:, rw:2 * rw]
    v_new = q[:, 2 * rw:3 * rw]
    wa = q[:, 3 * rw:3 * rw + PAIR]
    gl = q[:, 3 * rw + PAIR:]
    lane = lax.broadcasted_iota(jnp.int32, (1, PAIR), 1)
    wa = jnp.where(lane < HEAD, jnp.tanh(wa), wa)
    w_pre = _dot(wa, ww_ref[...])
    a_pre = _dot(wa, wa_ref[...])
    gate = _dot(jax.nn.sigmoid(gl), g2_ref[...])
    w0, a0, k_k, k_a, r_k = (pv_ref[i:i + 1, :] for i in range(5))

    def level(apow, tinv):
        sq = [_dot(jnp.concatenate([m, t], axis=0), bd_of(m)) for m, t in zip(apow, tinv)]
        return [rows(m, 0) for m in sq], [t + rows(m, 1) for t, m in zip(tinv, sq)]

    n_levels = 0
    while (4 << n_levels) < CHUNK:
        n_levels += 1
    for _ in range(n_levels // 2):
        apow, tinv = level(apow, tinv)
        chain_step()

    ld = jax.nn.sigmoid(w0 + w_pre) * NEG_EXP_M05
    a = jax.nn.sigmoid(a0 + a_pre)
    kk = k * k_k
    kk = kk / jnp.maximum(jnp.sqrt(_headsum1(kk * kk, bd)), 1e-12)
    k2 = k * (1.0 + (a - 1.0) * k_a)
    b = kk * a
    bonus = _headsum1(r * k2 * r_k, bd) * v_new

    for _ in range(n_levels - n_levels // 2):
        apow, tinv = level(apow, tinv)
        chain_step()
    tinv = [t + _dot(t, bd_of(m)) for m, t in zip(apow, tinv)]

    ri = lax.broadcasted_iota(jnp.int32, (tt, tt), 0)
    ci = lax.broadcasted_iota(jnp.int32, (tt, tt), 1)
    tril = jnp.where(((ri // CHUNK) == (ci // CHUNK)) & (ri >= ci), 1.0, 0.0).astype(BF16)
    cum = sum(jnp.dot(tril, part, preferred_element_type=F32) for part in _split3(ld))
    last = [cum[c * CHUNK + CHUNK - 1:(c + 1) * CHUNK, :] for c in range(nchunk)]
    einv = jnp.exp(-cum)

    vres = [_dot(jnp.concatenate([m1, m2, m3], axis=0), bd_of(ab(AB_V, u)))
            for m1, m2, m3, u in zip(aak, rk, ktt, units)]
    while pending:
        chain_step()

    rt_new = r * jnp.exp(cum)
    at_new = -kk * jnp.exp(cum - ld)
    bt_new = b * einv
    kt_new = k2 * einv

    wu = [_dot(t, jnp.concatenate([bd_of(ab(AB_AT, u)), bd_of(rows(m, 0))], axis=1))
          for t, m, u in zip(tinv, vres, units)]

    for qd in range(nquad):
        p_scr[qd] = state[qd]
    slot_out = lax.rem(step + 1, jnp.int32(3))
    y = y_scr[...]
    d = y - _headsum(y, bd) * (1.0 / HEAD)
    var = _headsum1(d * d, bd) * (1.0 / HEAD)
    yn = d * lax.rsqrt(var + LNX_EPS)
    yn = yn * lnx_ref[0:1, :] + lnx_ref[1:2, :] + bg_ref[slot_out, 0]
    yr_ref[0] = (yn * bg_ref[slot_out, 1]).astype(BF16)

    res = [_dot(jnp.concatenate([m, bt], axis=0),
                jnp.concatenate([bd_of(w[:, 0:QUAD]), bd_of(w[:, QUAD:])], axis=1))
           for m, bt, w in zip(rb, btt, wu)]
    for i, u in enumerate(units):
        c, qd = u
        g_row = gc_ref[c:c + 1, qd * QUAD:(qd + 1) * QUAD]
        g_cat = col_to_cat(jnp.sum(jnp.where(eye_bd, g_row, 0.0), axis=1, keepdims=True))
        sv_ref[SV_RH, i] = rt_prev(u) + rows(res[i], 0)[:, 0:QUAD]
        sv_ref[SV_YH, i] = rows(res[i], 0)[:, QUAD:] + rows(vres[i], 1)
        sv_ref[SV_BW, i] = g_cat * rows(res[i], 1)[:, 0:QUAD]
        sv_ref[SV_NT, i] = g_cat * (rows(res[i], 1)[:, QUAD:] + rows(vres[i], 2))
        sv_ref[SV_GC, i] = g_cat

    ab_ref[AB_AT] = at_new.astype(BF16)
    ab_ref[AB_BT] = bt_new.astype(BF16)
    ab_ref[AB_KT] = kt_new.astype(BF16)
    ab_ref[AB_V] = v_new.astype(BF16)
    rt_ref[...] = rt_new
    for c in range(nchunk):
        gc_ref[c:c + 1, :] = jnp.exp(last[c])
    slot_in = lax.rem(step, jnp.int32(3))
    bg_ref[slot_in, 0] = bonus
    bg_ref[slot_in, 1] = gate

    @pl.when(step < n_tiles)
    def _():
        yc_ref[0] = yc


def _mix_call(x, mod, win, cw, mu, pv, ww, wa, g2, bd, lnx, *, tt):
    bsz, t, d = x.shape
    ptot = win.shape[1]
    conv_w = cw.shape[1]
    rw = pv.shape[1]
    per_seq = t // tt
    n_tiles = bsz * per_seq
    assert tt // CHUNK <= 8, "gc scratch holds one row per chunk"

    def tile_at(lag):
        def index_map(i):
            j = jnp.clip(i - lag, 0, n_tiles - 1)
            return (j // per_seq, j % per_seq, 0)
        return index_map

    full = lambda a: pl.BlockSpec(a.shape, lambda i: (0,) * a.ndim)
    cur = lambda w: pl.BlockSpec((1, tt, w), tile_at(0))
    lag2 = lambda w: pl.BlockSpec((1, tt, w), tile_at(2))
    n_units = (tt // CHUNK) * (rw // QUAD)
    return pl.pallas_call(
        functools.partial(_mix_kernel, conv_w=conv_w, rw=rw, n_tiles=n_tiles, tiles_per_seq=per_seq),
        grid=(n_tiles + 2,),
        in_specs=[cur(d),
                  pl.BlockSpec((1,) + mod.shape[1:], lambda i: (jnp.minimum(i, n_tiles - 1) // per_seq, 0, 0)),
                  full(win), full(cw), full(mu), full(pv), full(ww), full(wa), full(g2), full(bd),
                  full(lnx)],
        out_specs=[cur(conv_w), lag2(rw)],
        out_shape=[jax.ShapeDtypeStruct((bsz, t, conv_w), BF16), jax.ShapeDtypeStruct((bsz, t, rw), BF16)],
        scratch_shapes=[pltpu.VMEM((8, conv_w), F32),
                        pltpu.VMEM((8, ptot - 3 * conv_w), F32),
                        pltpu.VMEM((rw // QUAD, HEAD, QUAD), F32),
                        pltpu.VMEM((tt, rw), F32),
                        pltpu.VMEM((5, n_units, CHUNK, QUAD), F32),
                        pltpu.VMEM((3, 2, tt, rw), F32),
                        pltpu.VMEM((4, tt, rw), BF16),
                        pltpu.VMEM((tt, rw), F32),
                        pltpu.VMEM((8, rw), F32)],
        compiler_params=pltpu.CompilerParams(dimension_semantics=("arbitrary",),
                                             vmem_limit_bytes=VMEM_LIMIT),
        name="mix",
    )(x, mod, win, cw, mu, pv, ww, wa, g2, bd, lnx)


def _outmlp_kernel(x_ref, yc_ref, yr_ref, mod_ref, wo_ref, wup_ref, wdn_ref, fg_ref, o_ref, *, ff_block):
    cw = yc_ref.shape[2]
    y = (jnp.dot(yc_ref[0], wo_ref[0:cw, :], preferred_element_type=F32)
         + jnp.dot(yr_ref[0], wo_ref[cw:, :], preferred_element_type=F32))
    x1 = x_ref[0] + mod_ref[0, 2:3, :] * y
    h = (_rms(x1) * (1.0 + mod_ref[0, 4:5, :]) + mod_ref[0, 3:4, :]).astype(BF16)
    f = jnp.zeros_like(x1)
    for j in range(wup_ref.shape[1] // ff_block):
        hj = jnp.dot(h, wup_ref[:, j * ff_block:(j + 1) * ff_block], preferred_element_type=F32)
        hj = jnp.square(jnp.maximum(hj, 0.0)).astype(BF16)
        f = f + jnp.dot(hj, wdn_ref[j * ff_block:(j + 1) * ff_block, :], preferred_element_type=F32)
    x2 = x1 + mod_ref[0, 5:6, :] * f
    o_ref[0] = _rms(x2) * fg_ref[...]


def _outmlp_call(x, yc, yr, mod, wo, wup, wdn, fg, *, tt):
    bsz, t, d = x.shape
    tok = lambda w: pl.BlockSpec((1, tt, w), lambda b, i: (b, i, 0))
    res = lambda a: pl.BlockSpec(a.shape, lambda b, i: (0,) * a.ndim, pipeline_mode=pl.Buffered(1))
    return pl.pallas_call(
        functools.partial(_outmlp_kernel, ff_block=1024),
        grid=(bsz, t // tt),
        in_specs=[tok(d), tok(yc.shape[2]), tok(yr.shape[2]),
                  pl.BlockSpec((1,) + mod.shape[1:], lambda b, i: (b, 0, 0)),
                  res(wo), res(wup), res(wdn), res(fg)],
        out_specs=tok(d),
        out_shape=jax.ShapeDtypeStruct((bsz, t, d), F32),
        compiler_params=pltpu.CompilerParams(dimension_semantics=("arbitrary", "arbitrary"),
                                             vmem_limit_bytes=VMEM_LIMIT),
        name="outmlp",
    )(x, yc, yr, mod, wo, wup, wdn, fg)


def kernel(x, c, w_ada, b_ada, w_in, conv_w, rwkv_mu, w0, w2, a0, a2, g2, k_k, k_a, r_k,
           lnx_g, lnx_b, w_out, w_up, w_down, final_g):
    assert w_ada.shape[0] == 1, "single-layer block"
    bsz, t, d = x.shape
    rw = w0.shape[1]
    n_mod = w_ada.shape[2] // d

    mod = _mod_call(c, w_ada[0], b_ada).reshape(bsz, n_mod, d)

    zpad = jnp.zeros((PAIR - w2.shape[1], rw), F32)
    ww = jnp.concatenate([w2[0], zpad], axis=0).astype(BF16)
    wa = jnp.concatenate([zpad, a2[0]], axis=0).astype(BF16)
    pv = jnp.concatenate([w0, a0, k_k, k_a, r_k.reshape(1, rw), jnp.zeros((3, rw), F32)], axis=0)
    hid = jnp.arange(QUAD, dtype=jnp.int32) // HEAD
    bd = (hid[:, None] == hid[None, :]).astype(BF16)
    lnx = jnp.concatenate([lnx_g, lnx_b], axis=0)

    yc, yr = _mix_call(x, mod, w_in[0].astype(BF16), conv_w[0, :, 0, :], rwkv_mu, pv, ww, wa,
                       g2[0].astype(BF16), bd, lnx, tt=256)
    return _outmlp_call(x, yc, yr, mod, w_out[0].astype(BF16), w_up[0].astype(BF16),
                        w_down[0].astype(BF16), final_g.reshape(1, d), tt=512)
```

```python
import functools

import jax
import jax.numpy as jnp
from jax import lax
from jax.experimental import pallas as pl
from jax.experimental.pallas import tpu as pltpu

F32 = jnp.float32
BF16 = jnp.bfloat16

HEAD = 64
CHUNK = 64
PAIR = 2 * HEAD
QUAD = 4 * HEAD
EPS = 1e-6
LNX_EPS = 1e-5 * HEAD
NEG_EXP_M05 = -0.6065306597126334
VMEM_LIMIT = 56 * 1024 * 1024


def _dot(a, b):
    return jnp.dot(a.astype(BF16), b.astype(BF16), preferred_element_type=F32)


def _dot_f32(a, b):
    return jnp.dot(a, b, precision=lax.Precision.HIGHEST, preferred_element_type=F32)


def _split2(x):
    hi = x.astype(BF16)
    lo = (x - hi.astype(F32)).astype(BF16)
    return hi, lo


def _quads_to_rows(x):
    return jnp.concatenate([x[:, i:i + QUAD] for i in range(0, x.shape[1], QUAD)], axis=0)


def _rows_to_quads(x, t):
    return jnp.concatenate([x[i:i + t] for i in range(0, x.shape[0], t)], axis=1)


def _headsum1(x, bd):
    s = jnp.dot(_quads_to_rows(x.astype(BF16)), bd, preferred_element_type=F32)
    return _rows_to_quads(s, x.shape[0])


def _headsum(x, bd):
    hi, lo = _split2(x)
    s = jnp.dot(_quads_to_rows(jnp.concatenate([hi, lo], axis=0)), bd, preferred_element_type=F32)
    s = _rows_to_quads(s, 2 * x.shape[0])
    return s[0:x.shape[0]] + s[x.shape[0]:]


def _rms(x):
    return x * lax.rsqrt(jnp.mean(x * x, axis=-1, keepdims=True) + EPS)


def _mod_kernel(c_ref, w_ref, b_ref, o_ref):
    c = c_ref[...]
    cond = c * jax.nn.sigmoid(c)
    o_ref[...] = _dot_f32(cond, w_ref[...]) + b_ref[...]


def _mod_call(c, w_ada, b_ada):
    bsz, d = c.shape
    n = w_ada.shape[1]
    bn = 1024
    return pl.pallas_call(
        _mod_kernel,
        grid=(n // bn,),
        in_specs=[pl.BlockSpec((bsz, d), lambda j: (0, 0)),
                  pl.BlockSpec((d, bn), lambda j: (0, j)),
                  pl.BlockSpec((1, bn), lambda j: (0, j))],
        out_specs=pl.BlockSpec((bsz, bn), lambda j: (0, j)),
        out_shape=jax.ShapeDtypeStruct((bsz, n), F32),
        compiler_params=pltpu.CompilerParams(dimension_semantics=("arbitrary",),
                                             vmem_limit_bytes=VMEM_LIMIT),
        name="mod",
    )(c, w_ada, b_ada)


SV_RH, SV_YH, SV_BW, SV_NT, SV_GC = range(5)
AB_AT, AB_V = range(2)
GT_BT, GT_KT = range(2)


def _mix_kernel(x_ref, mod_ref, win_ref, cw_ref, mu_ref, pv_ref, ww_ref, wa_ref, g2_ref, bd_ref, lnx_ref,
                yc_ref, yr_ref,
                zc_ref, qc_ref, p_scr, y_scr, sv_ref, bg_ref, ab_ref, rt_ref, gc_ref, gt_ref, tc_ref,
                *, conv_w, rw, n_tiles, tiles_per_seq):
    tt = x_ref.shape[1]
    step = pl.program_id(0)
    seq_start = lax.rem(step, jnp.int32(tiles_per_seq)) == 0
    chain_seq_start = lax.rem(step - 2, jnp.int32(tiles_per_seq)) == 0
    nquad = rw // QUAD
    nheads = QUAD // HEAD
    nchunk = tt // CHUNK

    @pl.when(step == 0)
    def _():
        for ref in (zc_ref, qc_ref, p_scr, sv_ref, bg_ref, ab_ref, rt_ref, gc_ref, gt_ref, tc_ref):
            ref[...] = jnp.zeros_like(ref)

    bd = bd_ref[...]
    ti = lax.broadcasted_iota(jnp.int32, (CHUNK, QUAD), 0)
    li = lax.broadcasted_iota(jnp.int32, (CHUNK, QUAD), 1)
    strict = ti > (li % HEAD)
    incl = ti >= (li % HEAD)
    eye_cat = jnp.where(ti == (li % HEAD), 1.0, 0.0).astype(F32)
    head_of_lane = li // HEAD
    qi = lax.broadcasted_iota(jnp.int32, (QUAD, QUAD), 0)
    qj = lax.broadcasted_iota(jnp.int32, (QUAD, QUAD), 1)
    eye_bd = qi == qj

    def bd_of(m):
        mb = m.astype(BF16)
        zero = jnp.zeros_like(mb)
        return jnp.concatenate([jnp.where(head_of_lane == hh, mb, zero) for hh in range(nheads)], axis=0)

    def col_to_cat(col):
        out = col[0:HEAD]
        for hh in range(1, nheads):
            out = jnp.where(head_of_lane == hh, col[hh * HEAD:(hh + 1) * HEAD], out)
        return out

    units = [(c, qd) for c in range(nchunk) for qd in range(nquad)]

    def rows(m, i):
        return m[i * CHUNK:(i + 1) * CHUNK]

    def ab(kind, u):
        c, qd = u
        return ab_ref[kind, c * CHUNK:(c + 1) * CHUNK, qd * QUAD:(qd + 1) * QUAD]

    def rt_prev(u):
        c, qd = u
        return rt_ref[c * CHUNK:(c + 1) * CHUNK, qd * QUAD:(qd + 1) * QUAD]

    state = [jnp.where(chain_seq_start, 0.0, p_scr[qd]) for qd in range(nquad)]
    pending = list(range(nchunk))

    def chain_step():
        if not pending:
            return
        c = pending.pop(0)
        for qd in range(nquad):
            i = c * nquad + qd
            pp = state[qd]
            m = _dot(jnp.concatenate([sv_ref[SV_RH, i], sv_ref[SV_BW, i]], axis=0), bd_of(pp))
            y_scr[c * CHUNK:(c + 1) * CHUNK, qd * QUAD:(qd + 1) * QUAD] = rows(m, 0) + sv_ref[SV_YH, i]
            state[qd] = sv_ref[SV_GC, i] * pp + rows(m, 1) + sv_ref[SV_NT, i]

    x = x_ref[0]
    h = _rms(x) * (1.0 + mod_ref[0, 1:2, :]) + mod_ref[0, 0:1, :]
    p = jnp.dot(h.astype(BF16), win_ref[...], preferred_element_type=F32)
    first = lax.broadcasted_iota(jnp.int32, (8, 1), 0) == 0
    cw = conv_w

    def shift_rows(m, carry_row):
        rolled = pltpu.roll(m, 1, 0)
        return jnp.concatenate([jnp.where(first, carry_row, rolled[0:8]), rolled[8:]], axis=0)

    gram = [_dot(jnp.concatenate([ab(AB_AT, u), rt_prev(u).astype(BF16)], axis=0),
                 jnp.concatenate([gt_ref[i, GT_BT], gt_ref[i, GT_KT]], axis=1))
            for i, u in enumerate(units)]
    chain_step()

    z = p[:, cw:2 * cw] * p[:, 2 * cw:3 * cw]
    z1 = shift_rows(z, jnp.where(seq_start, 0.0, zc_ref[0:1, :]))
    z2 = shift_rows(z1, jnp.where(seq_start, 0.0, zc_ref[1:2, :]))
    zc_ref[0:1, :] = z[tt - 1:tt, :]
    zc_ref[1:2, :] = z1[tt - 1:tt, :]
    yc = (p[:, 0:cw] * (cw_ref[0:1, :] * z2 + cw_ref[1:2, :] * z1 + cw_ref[2:3, :] * z)).astype(BF16)

    aab = [jnp.where(strict, rows(m, 0)[:, 0:QUAD], 0.0) for m in gram]
    aak = [jnp.where(strict, rows(m, 0)[:, QUAD:], 0.0) for m in gram]
    rb = [jnp.where(incl, rows(m, 1)[:, 0:QUAD], 0.0) for m in gram]
    rk = [jnp.where(incl, rows(m, 1)[:, QUAD:], 0.0) for m in gram]
    btt = [tc_ref[i, GT_BT] for i in range(len(units))]
    ktt = [tc_ref[i, GT_KT] for i in range(len(units))]
    tinv = [eye_cat + m for m in aab]
    apow = [_dot(m, bd_of(m)) for m in aab]

    q = p[:, 3 * cw:]
    qprev = shift_rows(q, jnp.where(seq_start, 0.0, qc_ref[0:1, :]))
    qc_ref[0:1, :] = q[tt - 1:tt, :]
    q = q + (qprev - q) * mu_ref[...]
    r = q[:, 0:rw]
    k = q[:, rw:2 * rw]
    v_new = q[:, 2 * rw:3 * rw]
    wa = q[:, 3 * rw:3 * rw + PAIR]
    gl = q[:, 3 * rw + PAIR:]
    lane = lax.broadcasted_iota(jnp.int32, (1, PAIR), 1)
    wa = jnp.where(lane < HEAD, jnp.tanh(wa), wa)
    w_pre = _dot(wa, ww_ref[...])
    a_pre = _dot(wa, wa_ref[...])
    gate = _dot(jax.nn.sigmoid(gl), g2_ref[...])
    w0, a0, k_k, k_a, r_k = (pv_ref[i:i + 1, :] for i in range(5))

    def level(apow, tinv):
        sq = [_dot(jnp.concatenate([m, t], axis=0), bd_of(m)) for m, t in zip(apow, tinv)]
        return [rows(m, 0) for m in sq], [t + rows(m, 1) for t, m in zip(tinv, sq)]

    n_levels = 0
    while (4 << n_levels) < CHUNK:
        n_levels += 1
    for _ in range(n_levels // 2):
        apow, tinv = level(apow, tinv)
        chain_step()

    ld = jax.nn.sigmoid(w0 + w_pre) * NEG_EXP_M05
    a = jax.nn.sigmoid(a0 + a_pre)
    kk = k * k_k
    kk = kk / jnp.maximum(jnp.sqrt(_headsum1(kk * kk, bd)), 1e-12)
    k2 = k * (1.0 + (a - 1.0) * k_a)
    b = kk * a
    bonus = _headsum1(r * k2 * r_k, bd) * v_new

    for _ in range(n_levels - n_levels // 2):
        apow, tinv = level(apow, tinv)
        chain_step()
    tinv = [t + _dot(t, bd_of(m)) for m, t in zip(apow, tinv)]

    ri = lax.broadcasted_iota(jnp.int32, (tt, tt), 0)
    ci = lax.broadcasted_iota(jnp.int32, (tt, tt), 1)
    tril = jnp.where(((ri // CHUNK) == (ci // CHUNK)) & (ri >= ci), 1.0, 0.0).astype(BF16)
    cum = sum(jnp.dot(tril, part, preferred_element_type=F32) for part in _split2(ld))
    last = [cum[c * CHUNK + CHUNK - 1:(c + 1) * CHUNK, :] for c in range(nchunk)]
    einv = jnp.exp(-cum)

    vres = [_dot(jnp.concatenate([m1, m2, m3], axis=0), bd_of(ab(AB_V, u)))
            for m1, m2, m3, u in zip(aak, rk, ktt, units)]
    while pending:
        chain_step()

    rt_new = r * jnp.exp(cum)
    at_new = -kk * jnp.exp(cum - ld)
    bt_new = b * einv
    kt_new = k2 * einv

    lt = [_dot(jnp.concatenate([m, bt], axis=0), bd_of(t)) for m, bt, t in zip(rb, btt, tinv)]

    for qd in range(nquad):
        p_scr[qd] = state[qd]
    slot_out = lax.rem(step + 1, jnp.int32(3))
    y = y_scr[...]
    d = y - _headsum(y, bd) * (1.0 / HEAD)
    var = _headsum1(d * d, bd) * (1.0 / HEAD)
    yn = d * lax.rsqrt(var + LNX_EPS)
    yn = yn * lnx_ref[0:1, :] + lnx_ref[1:2, :] + bg_ref[slot_out, 0]
    yr_ref[0] = (yn * bg_ref[slot_out, 1]).astype(BF16)

    res = [_dot(m, jnp.concatenate([bd_of(ab(AB_AT, u)), bd_of(rows(x, 0))], axis=1))
           for m, x, u in zip(lt, vres, units)]
    for i, u in enumerate(units):
        c, qd = u
        g_row = gc_ref[c:c + 1, qd * QUAD:(qd + 1) * QUAD]
        g_cat = col_to_cat(jnp.sum(jnp.where(eye_bd, g_row, 0.0), axis=1, keepdims=True))
        sv_ref[SV_RH, i] = rt_prev(u) + rows(res[i], 0)[:, 0:QUAD]
        sv_ref[SV_YH, i] = rows(res[i], 0)[:, QUAD:] + rows(vres[i], 1)
        sv_ref[SV_BW, i] = g_cat * rows(res[i], 1)[:, 0:QUAD]
        sv_ref[SV_NT, i] = g_cat * (rows(res[i], 1)[:, QUAD:] + rows(vres[i], 2))
        sv_ref[SV_GC, i] = g_cat

    ab_ref[AB_AT] = at_new.astype(BF16)
    ab_ref[AB_V] = v_new.astype(BF16)
    for i, (c, qd) in enumerate(units):
        for kind, full_tile in ((GT_BT, bt_new), (GT_KT, kt_new)):
            xt = jnp.transpose(full_tile[c * CHUNK:(c + 1) * CHUNK, qd * QUAD:(qd + 1) * QUAD])
            for hh in range(nheads):
                blk_t = xt[hh * HEAD:(hh + 1) * HEAD, :]
                gt_ref[i, kind, hh * HEAD:(hh + 1) * HEAD, hh * CHUNK:(hh + 1) * CHUNK] = blk_t
                tc_ref[i, kind, :, hh * CHUNK:(hh + 1) * CHUNK] = blk_t
    rt_ref[...] = rt_new
    for c in range(nchunk):
        gc_ref[c:c + 1, :] = jnp.exp(last[c])
    slot_in = lax.rem(step, jnp.int32(3))
    bg_ref[slot_in, 0] = bonus
    bg_ref[slot_in, 1] = gate

    @pl.when(step < n_tiles)
    def _():
        yc_ref[0] = yc


def _mix_call(x, mod, win, cw, mu, pv, ww, wa, g2, bd, lnx, *, tt):
    bsz, t, d = x.shape
    ptot = win.shape[1]
    conv_w = cw.shape[1]
    rw = pv.shape[1]
    per_seq = t // tt
    n_tiles = bsz * per_seq
    assert tt // CHUNK <= 8, "gc scratch holds one row per chunk"

    def tile_at(lag):
        def index_map(i):
            j = jnp.clip(i - lag, 0, n_tiles - 1)
            return (j // per_seq, j % per_seq, 0)
        return index_map

    full = lambda a: pl.BlockSpec(a.shape, lambda i: (0,) * a.ndim)
    cur = lambda w: pl.BlockSpec((1, tt, w), tile_at(0))
    lag2 = lambda w: pl.BlockSpec((1, tt, w), tile_at(2))
    n_units = (tt // CHUNK) * (rw // QUAD)
    return pl.pallas_call(
        functools.partial(_mix_kernel, conv_w=conv_w, rw=rw, n_tiles=n_tiles, tiles_per_seq=per_seq),
        grid=(n_tiles + 2,),
        in_specs=[cur(d),
                  pl.BlockSpec((1,) + mod.shape[1:], lambda i: (jnp.minimum(i, n_tiles - 1) // per_seq, 0, 0)),
                  full(win), full(cw), full(mu), full(pv), full(ww), full(wa), full(g2), full(bd),
                  full(lnx)],
        out_specs=[cur(conv_w), lag2(rw)],
        out_shape=[jax.ShapeDtypeStruct((bsz, t, conv_w), BF16), jax.ShapeDtypeStruct((bsz, t, rw), BF16)],
        scratch_shapes=[pltpu.VMEM((8, conv_w), F32),
                        pltpu.VMEM((8, ptot - 3 * conv_w), F32),
                        pltpu.VMEM((rw // QUAD, HEAD, QUAD), F32),
                        pltpu.VMEM((tt, rw), F32),
                        pltpu.VMEM((5, n_units, CHUNK, QUAD), F32),
                        pltpu.VMEM((3, 2, tt, rw), F32),
                        pltpu.VMEM((2, tt, rw), BF16),
                        pltpu.VMEM((tt, rw), F32),
                        pltpu.VMEM((8, rw), F32),
                        pltpu.VMEM((n_units, 2, QUAD, QUAD), F32),
                        pltpu.VMEM((n_units, 2, CHUNK, QUAD), F32)],
        compiler_params=pltpu.CompilerParams(dimension_semantics=("arbitrary",),
                                             vmem_limit_bytes=VMEM_LIMIT),
        name="mix",
    )(x, mod, win, cw, mu, pv, ww, wa, g2, bd, lnx)


def _outmlp_kernel(x_ref, yc_ref, yr_ref, mod_ref, wo_ref, wup_ref, wdn_ref, fg_ref, o_ref, *, ff_block):
    cw = yc_ref.shape[2]
    y = (jnp.dot(yc_ref[0], wo_ref[0:cw, :], preferred_element_type=F32)
         + jnp.dot(yr_ref[0], wo_ref[cw:, :], preferred_element_type=F32))
    x1 = x_ref[0] + mod_ref[0, 2:3, :] * y
    h = (_rms(x1) * (1.0 + mod_ref[0, 4:5, :]) + mod_ref[0, 3:4, :]).astype(BF16)
    f = jnp.zeros_like(x1)
    for j in range(wup_ref.shape[1] // ff_block):
        hj = jnp.dot(h, wup_ref[:, j * ff_block:(j + 1) * ff_block], preferred_element_type=F32)
        hj = jnp.square(jnp.maximum(hj, 0.0)).astype(BF16)
        f = f + jnp.dot(hj, wdn_ref[j * ff_block:(j + 1) * ff_block, :], preferred_element_type=F32)
    x2 = x1 + mod_ref[0, 5:6, :] * f
    o_ref[0] = _rms(x2) * fg_ref[...]


def _outmlp_call(x, yc, yr, mod, wo, wup, wdn, fg, *, tt):
    bsz, t, d = x.shape
    tok = lambda w: pl.BlockSpec((1, tt, w), lambda b, i: (b, i, 0))
    res = lambda a: pl.BlockSpec(a.shape, lambda b, i: (0,) * a.ndim, pipeline_mode=pl.Buffered(1))
    return pl.pallas_call(
        functools.partial(_outmlp_kernel, ff_block=1024),
        grid=(bsz, t // tt),
        in_specs=[tok(d), tok(yc.shape[2]), tok(yr.shape[2]),
                  pl.BlockSpec((1,) + mod.shape[1:], lambda b, i: (b, 0, 0)),
                  res(wo), res(wup), res(wdn), res(fg)],
        out_specs=tok(d),
        out_shape=jax.ShapeDtypeStruct((bsz, t, d), F32),
        compiler_params=pltpu.CompilerParams(dimension_semantics=("arbitrary", "arbitrary"),
                                             vmem_limit_bytes=VMEM_LIMIT),
        name="outmlp",
    )(x, yc, yr, mod, wo, wup, wdn, fg)


def kernel(x, c, w_ada, b_ada, w_in, conv_w, rwkv_mu, w0, w2, a0, a2, g2, k_k, k_a, r_k,
           lnx_g, lnx_b, w_out, w_up, w_down, final_g):
    assert w_ada.shape[0] == 1, "single-layer block"
    bsz, t, d = x.shape
    rw = w0.shape[1]
    n_mod = w_ada.shape[2] // d

    mod = _mod_call(c, w_ada[0], b_ada).reshape(bsz, n_mod, d)

    zpad = jnp.zeros((PAIR - w2.shape[1], rw), F32)
    ww = jnp.concatenate([w2[0], zpad], axis=0).astype(BF16)
    wa = jnp.concatenate([zpad, a2[0]], axis=0).astype(BF16)
    pv = jnp.concatenate([w0, a0, k_k, k_a, r_k.reshape(1, rw), jnp.zeros((3, rw), F32)], axis=0)
    hid = jnp.arange(QUAD, dtype=jnp.int32) // HEAD
    bd = (hid[:, None] == hid[None, :]).astype(BF16)
    lnx = jnp.concatenate([lnx_g, lnx_b], axis=0)

    yc, yr = _mix_call(x, mod, w_in[0].astype(BF16), conv_w[0, :, 0, :], rwkv_mu, pv, ww, wa,
                       g2[0].astype(BF16), bd, lnx, tt=256)
    return _outmlp_call(x, yc, yr, mod, w_out[0].astype(BF16), w_up[0].astype(BF16),
                        w_down[0].astype(BF16), final_g.reshape(1, d), tt=512)
```

```python
import functools

import jax
import jax.numpy as jnp
from jax import lax
from jax.experimental import pallas as pl
from jax.experimental.pallas import tpu as pltpu

F32 = jnp.float32
BF16 = jnp.bfloat16

HEAD = 64
CHUNK = 64
PAIR = 2 * HEAD
QUAD = 4 * HEAD
EPS = 1e-6
LNX_EPS = 1e-5 * HEAD
NEG_EXP_M05 = -0.6065306597126334
VMEM_LIMIT = 56 * 1024 * 1024


def _dot(a, b):
    return jnp.dot(a.astype(BF16), b.astype(BF16), preferred_element_type=F32)


def _dot_f32(a, b):
    return jnp.dot(a, b, precision=lax.Precision.HIGHEST, preferred_element_type=F32)


def _split2(x):
    hi = x.astype(BF16)
    lo = (x - hi.astype(F32)).astype(BF16)
    return hi, lo


def _quads_to_rows(x):
    return jnp.concatenate([x[:, i:i + QUAD] for i in range(0, x.shape[1], QUAD)], axis=0)


def _rows_to_quads(x, t):
    return jnp.concatenate([x[i:i + t] for i in range(0, x.shape[0], t)], axis=1)


def _headsum1(x, bd):
    s = jnp.dot(_quads_to_rows(x.astype(BF16)), bd, preferred_element_type=F32)
    return _rows_to_quads(s, x.shape[0])


def _rms(x):
    return x * lax.rsqrt(jnp.mean(x * x, axis=-1, keepdims=True) + EPS)


def _mod_kernel(c_ref, w_ref, b_ref, o_ref):
    c = c_ref[...]
    cond = c * jax.nn.sigmoid(c)
    o_ref[...] = _dot_f32(cond, w_ref[...]) + b_ref[...]


def _mod_call(c, w_ada, b_ada):
    bsz, d = c.shape
    n = w_ada.shape[1]
    bn = 1024
    return pl.pallas_call(
        _mod_kernel,
        grid=(n // bn,),
        in_specs=[pl.BlockSpec((bsz, d), lambda j: (0, 0)),
                  pl.BlockSpec((d, bn), lambda j: (0, j)),
                  pl.BlockSpec((1, bn), lambda j: (0, j))],
        out_specs=pl.BlockSpec((bsz, bn), lambda j: (0, j)),
        out_shape=jax.ShapeDtypeStruct((bsz, n), F32),
        compiler_params=pltpu.CompilerParams(dimension_semantics=("arbitrary",),
                                             vmem_limit_bytes=VMEM_LIMIT),
        name="mod",
    )(c, w_ada, b_ada)


SV_RH, SV_YH, SV_BW, SV_NT, SV_GC = range(5)
AB_AT, AB_V = range(2)
GT_BT, GT_KT = range(2)


def _mix_kernel(x_ref, mod_ref, win_ref, cw_ref, mu_ref, pv_ref, ww_ref, wa_ref, g2_ref, bd_ref, lnx_ref,
                yc_ref, yr_ref,
                zc_ref, qc_ref, p_scr, y_scr, sv_ref, bg_ref, ab_ref, rt_ref, gc_ref, gt_ref, tc_ref,
                *, conv_w, rw, n_tiles, tiles_per_seq):
    tt = x_ref.shape[1]
    step = pl.program_id(0)
    seq_start = lax.rem(step, jnp.int32(tiles_per_seq)) == 0
    chain_seq_start = lax.rem(step - 2, jnp.int32(tiles_per_seq)) == 0
    nquad = rw // QUAD
    nheads = QUAD // HEAD
    nchunk = tt // CHUNK

    @pl.when(step == 0)
    def _():
        for ref in (zc_ref, qc_ref, p_scr, sv_ref, bg_ref, ab_ref, rt_ref, gc_ref, gt_ref, tc_ref):
            ref[...] = jnp.zeros_like(ref)

    bd = bd_ref[...]
    ti = lax.broadcasted_iota(jnp.int32, (CHUNK, QUAD), 0)
    li = lax.broadcasted_iota(jnp.int32, (CHUNK, QUAD), 1)
    strict = ti > (li % HEAD)
    incl = ti >= (li % HEAD)
    eye_cat = jnp.where(ti == (li % HEAD), 1.0, 0.0).astype(F32)
    head_of_lane = li // HEAD
    qi = lax.broadcasted_iota(jnp.int32, (QUAD, QUAD), 0)
    qj = lax.broadcasted_iota(jnp.int32, (QUAD, QUAD), 1)
    eye_bd = qi == qj

    def bd_of(m):
        mb = m.astype(BF16)
        zero = jnp.zeros_like(mb)
        return jnp.concatenate([jnp.where(head_of_lane == hh, mb, zero) for hh in range(nheads)], axis=0)

    def col_to_cat(col):
        out = col[0:HEAD]
        for hh in range(1, nheads):
            out = jnp.where(head_of_lane == hh, col[hh * HEAD:(hh + 1) * HEAD], out)
        return out

    units = [(c, qd) for c in range(nchunk) for qd in range(nquad)]

    def rows(m, i):
        return m[i * CHUNK:(i + 1) * CHUNK]

    def ab(kind, u):
        c, qd = u
        return ab_ref[kind, c * CHUNK:(c + 1) * CHUNK, qd * QUAD:(qd + 1) * QUAD]

    def rt_prev(u):
        c, qd = u
        return rt_ref[c * CHUNK:(c + 1) * CHUNK, qd * QUAD:(qd + 1) * QUAD]

    state = [jnp.where(chain_seq_start, 0.0, p_scr[qd]) for qd in range(nquad)]
    pending = list(range(nchunk))

    def chain_step():
        if not pending:
            return
        c = pending.pop(0)
        for qd in range(nquad):
            i = c * nquad + qd
            pp = state[qd]
            m = _dot(jnp.concatenate([sv_ref[SV_RH, i], sv_ref[SV_BW, i]], axis=0), bd_of(pp))
            y_scr[c * CHUNK:(c + 1) * CHUNK, qd * QUAD:(qd + 1) * QUAD] = rows(m, 0) + sv_ref[SV_YH, i]
            state[qd] = sv_ref[SV_GC, i] * pp + rows(m, 1) + sv_ref[SV_NT, i]

    x = x_ref[0]
    h = _rms(x) * (1.0 + mod_ref[0, 1:2, :]) + mod_ref[0, 0:1, :]
    p = jnp.dot(h.astype(BF16), win_ref[...], preferred_element_type=F32)
    first = lax.broadcasted_iota(jnp.int32, (8, 1), 0) == 0
    cw = conv_w

    def shift_rows(m, carry_row):
        rolled = pltpu.roll(m, 1, 0)
        return jnp.concatenate([jnp.where(first, carry_row, rolled[0:8]), rolled[8:]], axis=0)

    at_prev = [ab(AB_AT, u) for u in units]
    v_prev = [ab(AB_V, u) for u in units]
    rt_prevs = [rt_prev(u) for u in units]
    g_cats = [col_to_cat(jnp.sum(jnp.where(eye_bd, gc_ref[c:c + 1, qd * QUAD:(qd + 1) * QUAD], 0.0),
                                 axis=1, keepdims=True)) for c, qd in units]
    gram = [_dot(jnp.concatenate([at_prev[i], rt_prevs[i].astype(BF16)], axis=0),
                 jnp.concatenate([gt_ref[i, GT_BT], gt_ref[i, GT_KT]], axis=1))
            for i in range(len(units))]
    chain_step()

    z = p[:, cw:2 * cw] * p[:, 2 * cw:3 * cw]
    z1 = shift_rows(z, jnp.where(seq_start, 0.0, zc_ref[0:1, :]))
    z2 = shift_rows(z1, jnp.where(seq_start, 0.0, zc_ref[1:2, :]))
    zc_ref[0:1, :] = z[tt - 1:tt, :]
    zc_ref[1:2, :] = z1[tt - 1:tt, :]
    yc = (p[:, 0:cw] * (cw_ref[0:1, :] * z2 + cw_ref[1:2, :] * z1 + cw_ref[2:3, :] * z)).astype(BF16)

    aab = [jnp.where(strict, rows(m, 0)[:, 0:QUAD], 0.0) for m in gram]
    aak = [jnp.where(strict, rows(m, 0)[:, QUAD:], 0.0) for m in gram]
    rb = [jnp.where(incl, rows(m, 1)[:, 0:QUAD], 0.0) for m in gram]
    rk = [jnp.where(incl, rows(m, 1)[:, QUAD:], 0.0) for m in gram]
    btt = [tc_ref[i, GT_BT].astype(BF16) for i in range(len(units))]
    ktt = [tc_ref[i, GT_KT].astype(BF16) for i in range(len(units))]
    tinv = [eye_cat + m for m in aab]
    apow = [_dot(m, bd_of(m)) for m in aab]

    q = p[:, 3 * cw:]
    qprev = shift_rows(q, jnp.where(seq_start, 0.0, qc_ref[0:1, :]))
    qc_ref[0:1, :] = q[tt - 1:tt, :]
    q = q + (qprev - q) * mu_ref[...]
    r = q[:, 0:rw]
    k = q[:, rw:2 * rw]
    v_new = q[:, 2 * rw:3 * rw]
    wa = q[:, 3 * rw:3 * rw + PAIR]
    gl = q[:, 3 * rw + PAIR:]
    lane = lax.broadcasted_iota(jnp.int32, (1, PAIR), 1)
    wa = jnp.where(lane < HEAD, jnp.tanh(wa), wa)
    w_pre = _dot(wa, ww_ref[...])
    a_pre = _dot(wa, wa_ref[...])
    gate = _dot(jax.nn.sigmoid(gl), g2_ref[...])
    w0, a0, k_k, k_a, r_k = (pv_ref[i:i + 1, :] for i in range(5))

    def level(apow, tinv):
        sq = [_dot(jnp.concatenate([m, t], axis=0), bd_of(m)) for m, t in zip(apow, tinv)]
        return [rows(m, 0) for m in sq], [t + rows(m, 1) for t, m in zip(tinv, sq)]

    n_levels = 0
    while (4 << n_levels) < CHUNK:
        n_levels += 1
    for _ in range(n_levels // 2):
        apow, tinv = level(apow, tinv)
        chain_step()

    ld = jax.nn.sigmoid(w0 + w_pre) * NEG_EXP_M05
    a = jax.nn.sigmoid(a0 + a_pre)
    kk = k * k_k
    kk = kk / jnp.maximum(jnp.sqrt(_headsum1(kk * kk, bd)), 1e-12)
    k2 = k * (1.0 + (a - 1.0) * k_a)
    b = kk * a
    bonus = _headsum1(r * k2 * r_k, bd) * v_new

    for _ in range(n_levels - n_levels // 2):
        apow, tinv = level(apow, tinv)
        chain_step()
    tinv = [t + _dot(t, bd_of(m)) for m, t in zip(apow, tinv)]

    ri = lax.broadcasted_iota(jnp.int32, (tt, tt), 0)
    ci = lax.broadcasted_iota(jnp.int32, (tt, tt), 1)
    tril = jnp.where(((ri // CHUNK) == (ci // CHUNK)) & (ri >= ci), 1.0, 0.0).astype(BF16)
    cum = sum(jnp.dot(tril, part, preferred_element_type=F32) for part in _split2(ld))
    last = [cum[c * CHUNK + CHUNK - 1:(c + 1) * CHUNK, :] for c in range(nchunk)]
    einv = jnp.exp(-cum)

    vres = [_dot(jnp.concatenate([m1.astype(BF16), m2.astype(BF16), m3], axis=0), bd_of(v))
            for m1, m2, m3, v in zip(aak, rk, ktt, v_prev)]
    while pending:
        chain_step()

    bt_new = b * einv
    kt_new = k2 * einv
    rt_ref[...] = r * jnp.exp(cum)
    ab_ref[AB_AT] = (-kk * jnp.exp(cum - ld)).astype(BF16)
    ab_ref[AB_V] = v_new.astype(BF16)
    for c in range(nchunk):
        gc_ref[c:c + 1, :] = jnp.exp(last[c])
    slot_in = lax.rem(step, jnp.int32(3))
    bg_ref[slot_in, 0] = bonus
    bg_ref[slot_in, 1] = gate
    for i, (c, qd) in enumerate(units):
        for kind, full_tile in ((GT_BT, bt_new), (GT_KT, kt_new)):
            xt = jnp.transpose(full_tile[c * CHUNK:(c + 1) * CHUNK, qd * QUAD:(qd + 1) * QUAD])
            for hh in range(nheads):
                blk_t = xt[hh * HEAD:(hh + 1) * HEAD, :]
                gt_ref[i, kind, hh * HEAD:(hh + 1) * HEAD, hh * CHUNK:(hh + 1) * CHUNK] = blk_t
                tc_ref[i, kind, :, hh * CHUNK:(hh + 1) * CHUNK] = blk_t

    lt = [_dot(jnp.concatenate([m, bt], axis=0), bd_of(t)) for m, bt, t in zip(rb, btt, tinv)]

    for qd in range(nquad):
        p_scr[qd] = state[qd]
    slot_out = lax.rem(step + 1, jnp.int32(3))
    y = y_scr[...]
    d = y - _headsum1(y, bd) * (1.0 / HEAD)
    var = _headsum1(d * d, bd) * (1.0 / HEAD)
    yn = d * lax.rsqrt(var + LNX_EPS)
    yn = yn * lnx_ref[0:1, :] + lnx_ref[1:2, :] + bg_ref[slot_out, 0]
    yr_ref[0] = (yn * bg_ref[slot_out, 1]).astype(BF16)

    res = [_dot(m, jnp.concatenate([bd_of(a_), bd_of(rows(x, 0))], axis=1))
           for m, x, a_ in zip(lt, vres, at_prev)]
    for i in range(len(units)):
        g_cat = g_cats[i]
        sv_ref[SV_RH, i] = rt_prevs[i] + rows(res[i], 0)[:, 0:QUAD]
        sv_ref[SV_YH, i] = rows(res[i], 0)[:, QUAD:] + rows(vres[i], 1)
        sv_ref[SV_BW, i] = g_cat * rows(res[i], 1)[:, 0:QUAD]
        sv_ref[SV_NT, i] = g_cat * (rows(res[i], 1)[:, QUAD:] + rows(vres[i], 2))
        sv_ref[SV_GC, i] = g_cat

    @pl.when(step < n_tiles)
    def _():
        yc_ref[0] = yc


def _mix_call(x, mod, win, cw, mu, pv, ww, wa, g2, bd, lnx, *, tt):
    bsz, t, d = x.shape
    ptot = win.shape[1]
    conv_w = cw.shape[1]
    rw = pv.shape[1]
    per_seq = t // tt
    n_tiles = bsz * per_seq
    assert tt // CHUNK <= 8, "gc scratch holds one row per chunk"

    def tile_at(lag):
        def index_map(i):
            j = jnp.clip(i - lag, 0, n_tiles - 1)
            return (j // per_seq, j % per_seq, 0)
        return index_map

    full = lambda a: pl.BlockSpec(a.shape, lambda i: (0,) * a.ndim)
    cur = lambda w: pl.BlockSpec((1, tt, w), tile_at(0))
    lag2 = lambda w: pl.BlockSpec((1, tt, w), tile_at(2))
    n_units = (tt // CHUNK) * (rw // QUAD)
    return pl.pallas_call(
        functools.partial(_mix_kernel, conv_w=conv_w, rw=rw, n_tiles=n_tiles, tiles_per_seq=per_seq),
        grid=(n_tiles + 2,),
        in_specs=[cur(d),
                  pl.BlockSpec((1,) + mod.shape[1:], lambda i: (jnp.minimum(i, n_tiles - 1) // per_seq, 0, 0)),
                  full(win), full(cw), full(mu), full(pv), full(ww), full(wa), full(g2), full(bd),
                  full(lnx)],
        out_specs=[cur(conv_w), lag2(rw)],
        out_shape=[jax.ShapeDtypeStruct((bsz, t, conv_w), BF16), jax.ShapeDtypeStruct((bsz, t, rw), BF16)],
        scratch_shapes=[pltpu.VMEM((8, conv_w), F32),
                        pltpu.VMEM((8, ptot - 3 * conv_w), F32),
                        pltpu.VMEM((rw // QUAD, HEAD, QUAD), F32),
                        pltpu.VMEM((tt, rw), F32),
                        pltpu.VMEM((5, n_units, CHUNK, QUAD), F32),
                        pltpu.VMEM((3, 2, tt, rw), F32),
                        pltpu.VMEM((2, tt, rw), BF16),
                        pltpu.VMEM((tt, rw), F32),
                        pltpu.VMEM((8, rw), F32),
                        pltpu.VMEM((n_units, 2, QUAD, QUAD), F32),
                        pltpu.VMEM((n_units, 2, CHUNK, QUAD), F32)],
        compiler_params=pltpu.CompilerParams(dimension_semantics=("arbitrary",),
                                             vmem_limit_bytes=VMEM_LIMIT),
        name="mix",
    )(x, mod, win, cw, mu, pv, ww, wa, g2, bd, lnx)


def _outmlp_kernel(x_ref, yc_ref, yr_ref, mod_ref, wo_ref, wup_ref, wdn_ref, fg_ref, o_ref, *, ff_block):
    cw = yc_ref.shape[2]
    y = (jnp.dot(yc_ref[0], wo_ref[0:cw, :], preferred_element_type=F32)
         + jnp.dot(yr_ref[0], wo_ref[cw:, :], preferred_element_type=F32))
    x1 = x_ref[0] + mod_ref[0, 2:3, :] * y
    h = (_rms(x1) * (1.0 + mod_ref[0, 4:5, :]) + mod_ref[0, 3:4, :]).astype(BF16)
    f = jnp.zeros_like(x1)
    for j in range(wup_ref.shape[1] // ff_block):
        hj = jnp.dot(h, wup_ref[:, j * ff_block:(j + 1) * ff_block], preferred_element_type=F32)
        hj = jnp.square(jnp.maximum(hj, 0.0)).astype(BF16)
        f = f + jnp.dot(hj, wdn_ref[j * ff_block:(j + 1) * ff_block, :], preferred_element_type=F32)
    x2 = x1 + mod_ref[0, 5:6, :] * f
    o_ref[0] = _rms(x2) * fg_ref[...]


def _outmlp_call(x, yc, yr, mod, wo, wup, wdn, fg, *, tt):
    bsz, t, d = x.shape
    tok = lambda w: pl.BlockSpec((1, tt, w), lambda b, i: (b, i, 0))
    res = lambda a: pl.BlockSpec(a.shape, lambda b, i: (0,) * a.ndim, pipeline_mode=pl.Buffered(1))
    return pl.pallas_call(
        functools.partial(_outmlp_kernel, ff_block=1024),
        grid=(bsz, t // tt),
        in_specs=[tok(d), tok(yc.shape[2]), tok(yr.shape[2]),
                  pl.BlockSpec((1,) + mod.shape[1:], lambda b, i: (b, 0, 0)),
                  res(wo), res(wup), res(wdn), res(fg)],
        out_specs=tok(d),
        out_shape=jax.ShapeDtypeStruct((bsz, t, d), F32),
        compiler_params=pltpu.CompilerParams(dimension_semantics=("arbitrary", "arbitrary"),
                                             vmem_limit_bytes=VMEM_LIMIT),
        name="outmlp",
    )(x, yc, yr, mod, wo, wup, wdn, fg)


def kernel(x, c, w_ada, b_ada, w_in, conv_w, rwkv_mu, w0, w2, a0, a2, g2, k_k, k_a, r_k,
           lnx_g, lnx_b, w_out, w_up, w_down, final_g):
    assert w_ada.shape[0] == 1, "single-layer block"
    bsz, t, d = x.shape
    rw = w0.shape[1]
    n_mod = w_ada.shape[2] // d

    mod = _mod_call(c, w_ada[0], b_ada).reshape(bsz, n_mod, d)

    zpad = jnp.zeros((PAIR - w2.shape[1], rw), F32)
    ww = jnp.concatenate([w2[0], zpad], axis=0).astype(BF16)
    wa = jnp.concatenate([zpad, a2[0]], axis=0).astype(BF16)
    pv = jnp.concatenate([w0, a0, k_k, k_a, r_k.reshape(1, rw), jnp.zeros((3, rw), F32)], axis=0)
    hid = jnp.arange(QUAD, dtype=jnp.int32) // HEAD
    bd = (hid[:, None] == hid[None, :]).astype(BF16)
    lnx = jnp.concatenate([lnx_g, lnx_b], axis=0)

    yc, yr = _mix_call(x, mod, w_in[0].astype(BF16), conv_w[0, :, 0, :], rwkv_mu, pv, ww, wa,
                       g2[0].astype(BF16), bd, lnx, tt=256)
    return _outmlp_call(x, yc, yr, mod, w_out[0].astype(BF16), w_up[0].astype(BF16),
                        w_down[0].astype(BF16), final_g.reshape(1, d), tt=512)
```

```python
import functools

import jax
import jax.numpy as jnp
from jax import lax
from jax.experimental import pallas as pl
from jax.experimental.pallas import tpu as pltpu

F32 = jnp.float32
BF16 = jnp.bfloat16

HEAD = 64
CHUNK = 64
PAIR = 2 * HEAD
QUAD = 4 * HEAD
EPS = 1e-6
LNX_EPS = 1e-5 * HEAD
NEG_EXP_M05 = -0.6065306597126334
VMEM_LIMIT = 56 * 1024 * 1024


def _dot(a, b):
    return jnp.dot(a.astype(BF16), b.astype(BF16), preferred_element_type=F32)


def _dot_f32(a, b):
    return jnp.dot(a, b, precision=lax.Precision.HIGHEST, preferred_element_type=F32)


def _split2(x):
    hi = x.astype(BF16)
    lo = (x - hi.astype(F32)).astype(BF16)
    return hi, lo


def _quads_to_rows(x):
    return jnp.concatenate([x[:, i:i + QUAD] for i in range(0, x.shape[1], QUAD)], axis=0)


def _rows_to_quads(x, t):
    return jnp.concatenate([x[i:i + t] for i in range(0, x.shape[0], t)], axis=1)


def _headsum1(x, bd):
    s = jnp.dot(_quads_to_rows(x.astype(BF16)), bd, preferred_element_type=F32)
    return _rows_to_quads(s, x.shape[0])


def _rms(x):
    return x * lax.rsqrt(jnp.mean(x * x, axis=-1, keepdims=True) + EPS)


def _mod_kernel(c_ref, w_ref, b_ref, o_ref):
    c = c_ref[...]
    cond = c * jax.nn.sigmoid(c)
    o_ref[...] = _dot_f32(cond, w_ref[...]) + b_ref[...]


def _mod_call(c, w_ada, b_ada):
    bsz, d = c.shape
    n = w_ada.shape[1]
    bn = 1024
    return pl.pallas_call(
        _mod_kernel,
        grid=(n // bn,),
        in_specs=[pl.BlockSpec((bsz, d), lambda j: (0, 0)),
                  pl.BlockSpec((d, bn), lambda j: (0, j)),
                  pl.BlockSpec((1, bn), lambda j: (0, j))],
        out_specs=pl.BlockSpec((bsz, bn), lambda j: (0, j)),
        out_shape=jax.ShapeDtypeStruct((bsz, n), F32),
        compiler_params=pltpu.CompilerParams(dimension_semantics=("arbitrary",),
                                             vmem_limit_bytes=VMEM_LIMIT),
        name="mod",
    )(c, w_ada, b_ada)


SV_RH, SV_YH, SV_BW, SV_NT, SV_GC = range(5)
AB_AT, AB_V = range(2)
GT_BT, GT_KT = range(2)


def _mix_kernel(x_ref, mod_ref, win_ref, cw_ref, mu_ref, pv_ref, ww_ref, wa_ref, g2_ref, bd_ref, lnx_ref,
                yc_ref, yr_ref,
                zc_ref, qc_ref, p_scr, y_scr, sv_ref, bg_ref, ab_ref, rt_ref, gc_ref, gt_ref, tc_ref,
                *, conv_w, rw, n_tiles, tiles_per_seq):
    tt = x_ref.shape[1]
    step = pl.program_id(0)
    seq_start = lax.rem(step, jnp.int32(tiles_per_seq)) == 0
    chain_seq_start = lax.rem(step - 2, jnp.int32(tiles_per_seq)) == 0
    nquad = rw // QUAD
    nheads = QUAD // HEAD
    nchunk = tt // CHUNK

    @pl.when(step == 0)
    def _():
        for ref in (zc_ref, qc_ref, p_scr, sv_ref, bg_ref, ab_ref, rt_ref, gc_ref, gt_ref, tc_ref):
            ref[...] = jnp.zeros_like(ref)

    bd = bd_ref[...]
    ti = lax.broadcasted_iota(jnp.int32, (CHUNK, QUAD), 0)
    li = lax.broadcasted_iota(jnp.int32, (CHUNK, QUAD), 1)
    strict = ti > (li % HEAD)
    incl = ti >= (li % HEAD)
    eye_cat = jnp.where(ti == (li % HEAD), 1.0, 0.0).astype(F32)
    head_of_lane = li // HEAD
    qi = lax.broadcasted_iota(jnp.int32, (QUAD, QUAD), 0)
    qj = lax.broadcasted_iota(jnp.int32, (QUAD, QUAD), 1)
    eye_bd = qi == qj

    def bd_of(m):
        mb = m.astype(BF16)
        zero = jnp.zeros_like(mb)
        return jnp.concatenate([jnp.where(head_of_lane == hh, mb, zero) for hh in range(nheads)], axis=0)

    def col_to_cat(col):
        out = col[0:HEAD]
        for hh in range(1, nheads):
            out = jnp.where(head_of_lane == hh, col[hh * HEAD:(hh + 1) * HEAD], out)
        return out

    units = [(c, qd) for c in range(nchunk) for qd in range(nquad)]

    def rows(m, i):
        return m[i * CHUNK:(i + 1) * CHUNK]

    def ab(kind, u):
        c, qd = u
        return ab_ref[kind, c * CHUNK:(c + 1) * CHUNK, qd * QUAD:(qd + 1) * QUAD]

    def rt_prev(u):
        c, qd = u
        return rt_ref[c * CHUNK:(c + 1) * CHUNK, qd * QUAD:(qd + 1) * QUAD]

    state = [jnp.where(chain_seq_start, 0.0, p_scr[qd]) for qd in range(nquad)]
    pending = list(range(nchunk))

    def chain_step():
        if not pending:
            return
        c = pending.pop(0)
        for qd in range(nquad):
            i = c * nquad + qd
            pp = state[qd]
            m = _dot(jnp.concatenate([sv_ref[SV_RH, i], sv_ref[SV_BW, i]], axis=0), bd_of(pp))
            y_scr[c * CHUNK:(c + 1) * CHUNK, qd * QUAD:(qd + 1) * QUAD] = rows(m, 0) + sv_ref[SV_YH, i]
            state[qd] = sv_ref[SV_GC, i] * pp + rows(m, 1) + sv_ref[SV_NT, i]

    x = x_ref[0]
    h = _rms(x) * (1.0 + mod_ref[0, 1:2, :]) + mod_ref[0, 0:1, :]
    p = jnp.dot(h.astype(BF16), win_ref[...], preferred_element_type=F32)
    first = lax.broadcasted_iota(jnp.int32, (8, 1), 0) == 0
    cw = conv_w

    def shift_rows(m, carry_row):
        rolled = pltpu.roll(m, 1, 0)
        return jnp.concatenate([jnp.where(first, carry_row, rolled[0:8]), rolled[8:]], axis=0)

    at_prev = [ab(AB_AT, u) for u in units]
    v_prev = [ab(AB_V, u) for u in units]
    rt_prevs = [rt_prev(u) for u in units]
    g_cats = [col_to_cat(jnp.sum(jnp.where(eye_bd, gc_ref[c:c + 1, qd * QUAD:(qd + 1) * QUAD], 0.0),
                                 axis=1, keepdims=True)) for c, qd in units]
    gram = [_dot(jnp.concatenate([at_prev[i], rt_prevs[i].astype(BF16)], axis=0),
                 jnp.concatenate([gt_ref[i, GT_BT], gt_ref[i, GT_KT]], axis=1))
            for i in range(len(units))]
    chain_step()

    z = p[:, cw:2 * cw] * p[:, 2 * cw:3 * cw]
    z1 = shift_rows(z, jnp.where(seq_start, 0.0, zc_ref[0:1, :]))
    z2 = shift_rows(z1, jnp.where(seq_start, 0.0, zc_ref[1:2, :]))
    zc_ref[0:1, :] = z[tt - 1:tt, :]
    zc_ref[1:2, :] = z1[tt - 1:tt, :]
    yc = (p[:, 0:cw] * (cw_ref[0:1, :] * z2 + cw_ref[1:2, :] * z1 + cw_ref[2:3, :] * z)).astype(BF16)

    aab = [jnp.where(strict, rows(m, 0)[:, 0:QUAD], 0.0) for m in gram]
    aak = [jnp.where(strict, rows(m, 0)[:, QUAD:], 0.0) for m in gram]
    rb = [jnp.where(incl, rows(m, 1)[:, 0:QUAD], 0.0) for m in gram]
    rk = [jnp.where(incl, rows(m, 1)[:, QUAD:], 0.0) for m in gram]
    btt = [tc_ref[i, GT_BT].astype(BF16) for i in range(len(units))]
    ktt = [tc_ref[i, GT_KT].astype(BF16) for i in range(len(units))]
    tinv = [eye_cat + m for m in aab]
    apow = [_dot(m, bd_of(m)) for m in aab]

    q = p[:, 3 * cw:]
    qprev = shift_rows(q, jnp.where(seq_start, 0.0, qc_ref[0:1, :]))
    qc_ref[0:1, :] = q[tt - 1:tt, :]
    q = q + (qprev - q) * mu_ref[...]
    r = q[:, 0:rw]
    k = q[:, rw:2 * rw]
    v_new = q[:, 2 * rw:3 * rw]
    wa = q[:, 3 * rw:3 * rw + PAIR]
    gl = q[:, 3 * rw + PAIR:]
    lane = lax.broadcasted_iota(jnp.int32, (1, PAIR), 1)
    wa = jnp.where(lane < HEAD, jnp.tanh(wa), wa)
    w_pre = _dot(wa, ww_ref[...])
    a_pre = _dot(wa, wa_ref[...])
    gate = _dot(jax.nn.sigmoid(gl), g2_ref[...])
    w0, a0, k_k, k_a, r_k = (pv_ref[i:i + 1, :] for i in range(5))

    def level(apow, tinv):
        sq = [_dot(jnp.concatenate([m, t], axis=0), bd_of(m)) for m, t in zip(apow, tinv)]
        return [rows(m, 0) for m in sq], [t + rows(m, 1) for t, m in zip(tinv, sq)]

    n_levels = 0
    while (4 << n_levels) < CHUNK:
        n_levels += 1
    for _ in range(n_levels // 2):
        apow, tinv = level(apow, tinv)
        chain_step()

    ld = jax.nn.sigmoid(w0 + w_pre) * NEG_EXP_M05
    a = jax.nn.sigmoid(a0 + a_pre)
    kk = k * k_k
    kk = kk / jnp.maximum(jnp.sqrt(_headsum1(kk * kk, bd)), 1e-12)
    k2 = k * (1.0 + (a - 1.0) * k_a)
    b = kk * a
    bonus = _headsum1(r * k2 * r_k, bd) * v_new

    for _ in range(n_levels - n_levels // 2):
        apow, tinv = level(apow, tinv)
        chain_step()
    tinv = [t + _dot(t, bd_of(m)) for m, t in zip(apow, tinv)]

    ri = lax.broadcasted_iota(jnp.int32, (tt, tt), 0)
    ci = lax.broadcasted_iota(jnp.int32, (tt, tt), 1)
    tril = jnp.where(((ri // CHUNK) == (ci // CHUNK)) & (ri >= ci), 1.0, 0.0).astype(BF16)
    cum = sum(jnp.dot(tril, part, preferred_element_type=F32) for part in _split2(ld))
    last = [cum[c * CHUNK + CHUNK - 1:(c + 1) * CHUNK, :] for c in range(nchunk)]
    einv = jnp.exp(-cum)

    vres = [_dot(jnp.concatenate([m1.astype(BF16), m2.astype(BF16), m3], axis=0), bd_of(v))
            for m1, m2, m3, v in zip(aak, rk, ktt, v_prev)]
    while pending:
        chain_step()

    bt_new = b * einv
    kt_new = k2 * einv
    rt_ref[...] = r * jnp.exp(cum)
    ab_ref[AB_AT] = (-kk * jnp.exp(cum - ld)).astype(BF16)
    ab_ref[AB_V] = v_new.astype(BF16)
    for c in range(nchunk):
        gc_ref[c:c + 1, :] = jnp.exp(last[c])
    slot_in = lax.rem(step, jnp.int32(3))
    bg_ref[slot_in, 0] = bonus
    bg_ref[slot_in, 1] = gate
    for i, (c, qd) in enumerate(units):
        for kind, full_tile in ((GT_BT, bt_new), (GT_KT, kt_new)):
            xt = jnp.transpose(full_tile[c * CHUNK:(c + 1) * CHUNK, qd * QUAD:(qd + 1) * QUAD])
            for hh in range(nheads):
                blk_t = xt[hh * HEAD:(hh + 1) * HEAD, :]
                gt_ref[i, kind, hh * HEAD:(hh + 1) * HEAD, hh * CHUNK:(hh + 1) * CHUNK] = blk_t
                tc_ref[i, kind, :, hh * CHUNK:(hh + 1) * CHUNK] = blk_t

    lt = [_dot(jnp.concatenate([m, bt], axis=0), bd_of(t)) for m, bt, t in zip(rb, btt, tinv)]

    for qd in range(nquad):
        p_scr[qd] = state[qd]
    slot_out = lax.rem(step + 1, jnp.int32(3))
    y = y_scr[...]
    d = y - _headsum1(y, bd) * (1.0 / HEAD)
    var = _headsum1(d * d, bd) * (1.0 / HEAD)
    yn = d * lax.rsqrt(var + LNX_EPS)
    yn = yn * lnx_ref[0:1, :] + lnx_ref[1:2, :] + bg_ref[slot_out, 0]
    yr_ref[0] = (yn * bg_ref[slot_out, 1]).astype(BF16)

    res = [_dot(m, jnp.concatenate([bd_of(a_), bd_of(rows(x, 0))], axis=1))
           for m, x, a_ in zip(lt, vres, at_prev)]
    for i in range(len(units)):
        g_cat = g_cats[i]
        sv_ref[SV_RH, i] = rt_prevs[i] + rows(res[i], 0)[:, 0:QUAD]
        sv_ref[SV_YH, i] = rows(res[i], 0)[:, QUAD:] + rows(vres[i], 1)
        sv_ref[SV_BW, i] = g_cat * rows(res[i], 1)[:, 0:QUAD]
        sv_ref[SV_NT, i] = g_cat * (rows(res[i], 1)[:, QUAD:] + rows(vres[i], 2))
        sv_ref[SV_GC, i] = g_cat

    @pl.when(step < n_tiles)
    def _():
        yc_ref[0] = yc


def _mix_call(x, mod, win, cw, mu, pv, ww, wa, g2, bd, lnx, *, tt):
    bsz, t, d = x.shape
    ptot = win.shape[1]
    conv_w = cw.shape[1]
    rw = pv.shape[1]
    per_seq = t // tt
    n_tiles = bsz * per_seq
    assert tt // CHUNK <= 8, "gc scratch holds one row per chunk"

    def tile_at(lag):
        def index_map(i):
            j = jnp.clip(i - lag, 0, n_tiles - 1)
            return (j // per_seq, j % per_seq, 0)
        return index_map

    full = lambda a: pl.BlockSpec(a.shape, lambda i: (0,) * a.ndim)
    cur = lambda w: pl.BlockSpec((1, tt, w), tile_at(0))
    lag2 = lambda w: pl.BlockSpec((1, tt, w), tile_at(2))
    n_units = (tt // CHUNK) * (rw // QUAD)
    return pl.pallas_call(
        functools.partial(_mix_kernel, conv_w=conv_w, rw=rw, n_tiles=n_tiles, tiles_per_seq=per_seq),
        grid=(n_tiles + 2,),
        in_specs=[cur(d),
                  pl.BlockSpec((1,) + mod.shape[1:], lambda i: (jnp.minimum(i, n_tiles - 1) // per_seq, 0, 0)),
                  full(win), full(cw), full(mu), full(pv), full(ww), full(wa), full(g2), full(bd),
                  full(lnx)],
        out_specs=[cur(conv_w), lag2(rw)],
        out_shape=[jax.ShapeDtypeStruct((bsz, t, conv_w), BF16), jax.ShapeDtypeStruct((bsz, t, rw), BF16)],
        scratch_shapes=[pltpu.VMEM((8, conv_w), F32),
                        pltpu.VMEM((8, ptot - 3 * conv_w), F32),
                        pltpu.VMEM((rw // QUAD, HEAD, QUAD), F32),
                        pltpu.VMEM((tt, rw), F32),
                        pltpu.VMEM((5, n_units, CHUNK, QUAD), F32),
                        pltpu.VMEM((3, 2, tt, rw), F32),
                        pltpu.VMEM((2, tt, rw), BF16),
                        pltpu.VMEM((tt, rw), F32),
                        pltpu.VMEM((8, rw), F32),
                        pltpu.VMEM((n_units, 2, QUAD, QUAD), F32),
                        pltpu.VMEM((n_units, 2, CHUNK, QUAD), F32)],
        compiler_params=pltpu.CompilerParams(dimension_semantics=("arbitrary",),
                                             vmem_limit_bytes=VMEM_LIMIT),
        name="mix",
    )(x, mod, win, cw, mu, pv, ww, wa, g2, bd, lnx)


def _outmlp_kernel(x_ref, yc_ref, yr_ref, mod_ref, wo_ref, wup_ref, wdn_ref, fg_ref, o_ref,
                   *, ff_block, n_split):
    cw = yc_ref.shape[2]
    sub = x_ref.shape[1] // n_split
    x1s, hs = [], []
    for g in range(n_split):
        rs = slice(g * sub, (g + 1) * sub)
        y = (jnp.dot(yc_ref[0, rs, :], wo_ref[0:cw, :], preferred_element_type=F32)
             + jnp.dot(yr_ref[0, rs, :], wo_ref[cw:, :], preferred_element_type=F32))
        x1 = x_ref[0, rs, :] + mod_ref[0, 2:3, :] * y
        x1s.append(x1)
        hs.append((_rms(x1) * (1.0 + mod_ref[0, 4:5, :]) + mod_ref[0, 3:4, :]).astype(BF16))
    for g in range(n_split):
        f = jnp.zeros_like(x1s[g])
        for j in range(wup_ref.shape[1] // ff_block):
            hj = jnp.dot(hs[g], wup_ref[:, j * ff_block:(j + 1) * ff_block], preferred_element_type=F32)
            hj = jnp.square(jnp.maximum(hj, 0.0)).astype(BF16)
            f = f + jnp.dot(hj, wdn_ref[j * ff_block:(j + 1) * ff_block, :], preferred_element_type=F32)
        x2 = x1s[g] + mod_ref[0, 5:6, :] * f
        o_ref[0, g * sub:(g + 1) * sub, :] = _rms(x2) * fg_ref[...]


def _outmlp_call(x, yc, yr, mod, wo, wup, wdn, fg, *, tt):
    bsz, t, d = x.shape
    tok = lambda w: pl.BlockSpec((1, tt, w), lambda b, i: (b, i, 0))
    res = lambda a: pl.BlockSpec(a.shape, lambda b, i: (0,) * a.ndim, pipeline_mode=pl.Buffered(1))
    return pl.pallas_call(
        functools.partial(_outmlp_kernel, ff_block=1024, n_split=4),
        grid=(bsz, t // tt),
        in_specs=[tok(d), tok(yc.shape[2]), tok(yr.shape[2]),
                  pl.BlockSpec((1,) + mod.shape[1:], lambda b, i: (b, 0, 0)),
                  res(wo), res(wup), res(wdn), res(fg)],
        out_specs=tok(d),
        out_shape=jax.ShapeDtypeStruct((bsz, t, d), F32),
        compiler_params=pltpu.CompilerParams(dimension_semantics=("arbitrary", "arbitrary"),
                                             vmem_limit_bytes=VMEM_LIMIT),
        name="outmlp",
    )(x, yc, yr, mod, wo, wup, wdn, fg)


def kernel(x, c, w_ada, b_ada, w_in, conv_w, rwkv_mu, w0, w2, a0, a2, g2, k_k, k_a, r_k,
           lnx_g, lnx_b, w_out, w_up, w_down, final_g):
    assert w_ada.shape[0] == 1, "single-layer block"
    bsz, t, d = x.shape
    rw = w0.shape[1]
    n_mod = w_ada.shape[2] // d

    mod = _mod_call(c, w_ada[0], b_ada).reshape(bsz, n_mod, d)

    zpad = jnp.zeros((PAIR - w2.shape[1], rw), F32)
    ww = jnp.concatenate([w2[0], zpad], axis=0).astype(BF16)
    wa = jnp.concatenate([zpad, a2[0]], axis=0).astype(BF16)
    pv = jnp.concatenate([w0, a0, k_k, k_a, r_k.reshape(1, rw), jnp.zeros((3, rw), F32)], axis=0)
    hid = jnp.arange(QUAD, dtype=jnp.int32) // HEAD
    bd = (hid[:, None] == hid[None, :]).astype(BF16)
    lnx = jnp.concatenate([lnx_g, lnx_b], axis=0)

    yc, yr = _mix_call(x, mod, w_in[0].astype(BF16), conv_w[0, :, 0, :], rwkv_mu, pv, ww, wa,
                       g2[0].astype(BF16), bd, lnx, tt=256)
    return _outmlp_call(x, yc, yr, mod, w_out[0].astype(BF16), w_up[0].astype(BF16),
                        w_down[0].astype(BF16), final_g.reshape(1, d), tt=1024)
```

```python
import functools

import jax
import jax.numpy as jnp
from jax import lax
from jax.experimental import pallas as pl
from jax.experimental.pallas import tpu as pltpu

F32 = jnp.float32
BF16 = jnp.bfloat16

HEAD = 64
CHUNK = 64
PAIR = 2 * HEAD
QUAD = 4 * HEAD
EPS = 1e-6
LNX_EPS = 1e-5 * HEAD
NEG_EXP_M05 = -0.6065306597126334
VMEM_LIMIT = 56 * 1024 * 1024


def _dot(a, b):
    return jnp.dot(a.astype(BF16), b.astype(BF16), preferred_element_type=F32)


def _dot_f32(a, b):
    return jnp.dot(a, b, precision=lax.Precision.HIGHEST, preferred_element_type=F32)


def _split2(x):
    hi = x.astype(BF16)
    lo = (x - hi.astype(F32)).astype(BF16)
    return hi, lo


def _quads_to_rows(x):
    return jnp.concatenate([x[:, i:i + QUAD] for i in range(0, x.shape[1], QUAD)], axis=0)


def _rows_to_quads(x, t):
    return jnp.concatenate([x[i:i + t] for i in range(0, x.shape[0], t)], axis=1)


def _headsum1(x, bd):
    s = jnp.dot(_quads_to_rows(x.astype(BF16)), bd, preferred_element_type=F32)
    return _rows_to_quads(s, x.shape[0])


def _rms(x):
    return x * lax.rsqrt(jnp.mean(x * x, axis=-1, keepdims=True) + EPS)


def _mod_kernel(c_ref, w_ref, b_ref, o_ref):
    c = c_ref[...]
    cond = c * jax.nn.sigmoid(c)
    o_ref[...] = _dot_f32(cond, w_ref[...]) + b_ref[...]


def _mod_call(c, w_ada, b_ada):
    bsz, d = c.shape
    n = w_ada.shape[1]
    bn = 1024
    return pl.pallas_call(
        _mod_kernel,
        grid=(n // bn,),
        in_specs=[pl.BlockSpec((bsz, d), lambda j: (0, 0)),
                  pl.BlockSpec((d, bn), lambda j: (0, j)),
                  pl.BlockSpec((1, bn), lambda j: (0, j))],
        out_specs=pl.BlockSpec((bsz, bn), lambda j: (0, j)),
        out_shape=jax.ShapeDtypeStruct((bsz, n), F32),
        compiler_params=pltpu.CompilerParams(dimension_semantics=("arbitrary",),
                                             vmem_limit_bytes=VMEM_LIMIT),
        name="mod",
    )(c, w_ada, b_ada)


SV_RH, SV_YH, SV_BW, SV_NT, SV_GC = range(5)
AB_AT, AB_V = range(2)
GT_BT, GT_KT = range(2)


def _mix_kernel(x_ref, mod_ref, win_ref, cw_ref, mu_ref, pv_ref, ww_ref, wa_ref, g2_ref, bd_ref, lnx_ref,
                yc_ref, yr_ref,
                zc_ref, qc_ref, p_scr, y_scr, sv_ref, bg_ref, ab_ref, rt_ref, gc_ref, gt_ref, tc_ref,
                *, conv_w, rw, n_tiles, tiles_per_seq):
    tt = x_ref.shape[1]
    step = pl.program_id(0)
    seq_start = lax.rem(step, jnp.int32(tiles_per_seq)) == 0
    chain_seq_start = lax.rem(step - 2, jnp.int32(tiles_per_seq)) == 0
    nquad = rw // QUAD
    nheads = QUAD // HEAD
    nchunk = tt // CHUNK

    @pl.when(step == 0)
    def _():
        for ref in (zc_ref, qc_ref, p_scr, sv_ref, bg_ref, ab_ref, rt_ref, gc_ref, gt_ref, tc_ref):
            ref[...] = jnp.zeros_like(ref)

    bd = bd_ref[...]
    ti = lax.broadcasted_iota(jnp.int32, (CHUNK, QUAD), 0)
    li = lax.broadcasted_iota(jnp.int32, (CHUNK, QUAD), 1)
    strict = ti > (li % HEAD)
    incl = ti >= (li % HEAD)
    eye_cat = jnp.where(ti == (li % HEAD), 1.0, 0.0).astype(F32)
    head_of_lane = li // HEAD
    qi = lax.broadcasted_iota(jnp.int32, (QUAD, QUAD), 0)
    qj = lax.broadcasted_iota(jnp.int32, (QUAD, QUAD), 1)
    eye_bd = qi == qj

    def bd_of(m):
        mb = m.astype(BF16)
        zero = jnp.zeros_like(mb)
        return jnp.concatenate([jnp.where(head_of_lane == hh, mb, zero) for hh in range(nheads)], axis=0)

    def col_to_cat(col):
        out = col[0:HEAD]
        for hh in range(1, nheads):
            out = jnp.where(head_of_lane == hh, col[hh * HEAD:(hh + 1) * HEAD], out)
        return out

    units = [(c, qd) for c in range(nchunk) for qd in range(nquad)]

    def rows(m, i):
        return m[i * CHUNK:(i + 1) * CHUNK]

    def ab(kind, u):
        c, qd = u
        return ab_ref[kind, c * CHUNK:(c + 1) * CHUNK, qd * QUAD:(qd + 1) * QUAD]

    def rt_prev(u):
        c, qd = u
        return rt_ref[c * CHUNK:(c + 1) * CHUNK, qd * QUAD:(qd + 1) * QUAD]

    state = [jnp.where(chain_seq_start, 0.0, p_scr[qd]) for qd in range(nquad)]
    pending = list(range(nchunk))

    def chain_step():
        if not pending:
            return
        c = pending.pop(0)
        for qd in range(nquad):
            i = c * nquad + qd
            pp = state[qd]
            m = _dot(jnp.concatenate([sv_ref[SV_RH, i], sv_ref[SV_BW, i]], axis=0), bd_of(pp))
            y_scr[c * CHUNK:(c + 1) * CHUNK, qd * QUAD:(qd + 1) * QUAD] = rows(m, 0) + sv_ref[SV_YH, i]
            state[qd] = sv_ref[SV_GC, i] * pp + rows(m, 1) + sv_ref[SV_NT, i]

    x = x_ref[0]
    h = _rms(x) * (1.0 + mod_ref[0, 1:2, :]) + mod_ref[0, 0:1, :]
    p = jnp.dot(h.astype(BF16), win_ref[...], preferred_element_type=F32)
    first = lax.broadcasted_iota(jnp.int32, (8, 1), 0) == 0
    cw = conv_w

    def shift_rows(m, carry_row):
        rolled = pltpu.roll(m, 1, 0)
        return jnp.concatenate([jnp.where(first, carry_row, rolled[0:8]), rolled[8:]], axis=0)

    at_prev = [ab(AB_AT, u) for u in units]
    v_prev = [ab(AB_V, u) for u in units]
    rt_prevs = [rt_prev(u) for u in units]
    g_cats = [col_to_cat(jnp.sum(jnp.where(eye_bd, gc_ref[c:c + 1, qd * QUAD:(qd + 1) * QUAD], 0.0),
                                 axis=1, keepdims=True)) for c, qd in units]
    gram = [_dot(jnp.concatenate([at_prev[i], rt_prevs[i].astype(BF16)], axis=0),
                 jnp.concatenate([gt_ref[i, GT_BT], gt_ref[i, GT_KT]], axis=1))
            for i in range(len(units))]
    chain_step()

    z = p[:, cw:2 * cw] * p[:, 2 * cw:3 * cw]
    z1 = shift_rows(z, jnp.where(seq_start, 0.0, zc_ref[0:1, :]))
    z2 = shift_rows(z1, jnp.where(seq_start, 0.0, zc_ref[1:2, :]))
    zc_ref[0:1, :] = z[tt - 1:tt, :]
    zc_ref[1:2, :] = z1[tt - 1:tt, :]
    yc = (p[:, 0:cw] * (cw_ref[0:1, :] * z2 + cw_ref[1:2, :] * z1 + cw_ref[2:3, :] * z)).astype(BF16)

    aab = [jnp.where(strict, rows(m, 0)[:, 0:QUAD], 0.0) for m in gram]
    aak = [jnp.where(strict, rows(m, 0)[:, QUAD:], 0.0) for m in gram]
    rb = [jnp.where(incl, rows(m, 1)[:, 0:QUAD], 0.0) for m in gram]
    rk = [jnp.where(incl, rows(m, 1)[:, QUAD:], 0.0) for m in gram]
    btt = [tc_ref[i, GT_BT].astype(BF16) for i in range(len(units))]
    ktt = [tc_ref[i, GT_KT].astype(BF16) for i in range(len(units))]
    tinv = [eye_cat + m for m in aab]
    apow = [_dot(m, bd_of(m)) for m in aab]

    q = p[:, 3 * cw:]
    qprev = shift_rows(q, jnp.where(seq_start, 0.0, qc_ref[0:1, :]))
    qc_ref[0:1, :] = q[tt - 1:tt, :]
    q = q + (qprev - q) * mu_ref[...]
    r = q[:, 0:rw]
    k = q[:, rw:2 * rw]
    v_new = q[:, 2 * rw:3 * rw]
    wa = q[:, 3 * rw:3 * rw + PAIR]
    gl = q[:, 3 * rw + PAIR:]
    lane = lax.broadcasted_iota(jnp.int32, (1, PAIR), 1)
    wa = jnp.where(lane < HEAD, jnp.tanh(wa), wa)
    w_pre = _dot(wa, ww_ref[...])
    a_pre = _dot(wa, wa_ref[...])
    gate = _dot(jax.nn.sigmoid(gl), g2_ref[...])
    w0, a0, k_k, k_a, r_k = (pv_ref[i:i + 1, :] for i in range(5))

    def level(apow, tinv):
        sq = [_dot(jnp.concatenate([m, t], axis=0), bd_of(m)) for m, t in zip(apow, tinv)]
        return [rows(m, 0) for m in sq], [t + rows(m, 1) for t, m in zip(tinv, sq)]

    n_levels = 0
    while (4 << n_levels) < CHUNK:
        n_levels += 1
    for _ in range(n_levels // 2):
        apow, tinv = level(apow, tinv)
        chain_step()

    ld = jax.nn.sigmoid(w0 + w_pre) * NEG_EXP_M05
    a = jax.nn.sigmoid(a0 + a_pre)
    kk = k * k_k
    kk = kk / jnp.maximum(jnp.sqrt(_headsum1(kk * kk, bd)), 1e-12)
    k2 = k * (1.0 + (a - 1.0) * k_a)
    b = kk * a
    bonus = _headsum1(r * k2 * r_k, bd) * v_new

    for _ in range(n_levels - n_levels // 2):
        apow, tinv = level(apow, tinv)
        chain_step()
    tinv = [t + _dot(t, bd_of(m)) for m, t in zip(apow, tinv)]

    ri = lax.broadcasted_iota(jnp.int32, (tt, tt), 0)
    ci = lax.broadcasted_iota(jnp.int32, (tt, tt), 1)
    tril = jnp.where(((ri // CHUNK) == (ci // CHUNK)) & (ri >= ci), 1.0, 0.0).astype(BF16)
    cum = sum(jnp.dot(tril, part, preferred_element_type=F32) for part in _split2(ld))
    last = [cum[c * CHUNK + CHUNK - 1:(c + 1) * CHUNK, :] for c in range(nchunk)]
    einv = jnp.exp(-cum)

    vres = [_dot(jnp.concatenate([m1.astype(BF16), m2.astype(BF16), m3], axis=0), bd_of(v))
            for m1, m2, m3, v in zip(aak, rk, ktt, v_prev)]
    while pending:
        chain_step()

    bt_new = b * einv
    kt_new = k2 * einv
    rt_ref[...] = r * jnp.exp(cum)
    ab_ref[AB_AT] = (-kk * jnp.exp(cum - ld)).astype(BF16)
    ab_ref[AB_V] = v_new.astype(BF16)
    for c in range(nchunk):
        gc_ref[c:c + 1, :] = jnp.exp(last[c])
    slot_in = lax.rem(step, jnp.int32(3))
    bg_ref[slot_in, 0] = bonus
    bg_ref[slot_in, 1] = gate
    for i, (c, qd) in enumerate(units):
        for kind, full_tile in ((GT_BT, bt_new), (GT_KT, kt_new)):
            xt = jnp.transpose(full_tile[c * CHUNK:(c + 1) * CHUNK, qd * QUAD:(qd + 1) * QUAD])
            for hh in range(nheads):
                blk_t = xt[hh * HEAD:(hh + 1) * HEAD, :]
                gt_ref[i, kind, hh * HEAD:(hh + 1) * HEAD, hh * CHUNK:(hh + 1) * CHUNK] = blk_t
                tc_ref[i, kind, :, hh * CHUNK:(hh + 1) * CHUNK] = blk_t

    lt = [_dot(jnp.concatenate([m, bt], axis=0), bd_of(t)) for m, bt, t in zip(rb, btt, tinv)]

    for qd in range(nquad):
        p_scr[qd] = state[qd]
    slot_out = lax.rem(step + 1, jnp.int32(3))
    y = y_scr[...]
    d = y - _headsum1(y, bd) * (1.0 / HEAD)
    var = _headsum1(d * d, bd) * (1.0 / HEAD)
    yn = d * lax.rsqrt(var + LNX_EPS)
    yn = yn * lnx_ref[0:1, :] + lnx_ref[1:2, :] + bg_ref[slot_out, 0]
    yr_ref[0] = (yn * bg_ref[slot_out, 1]).astype(BF16)

    res = [_dot(m, jnp.concatenate([bd_of(a_), bd_of(rows(x, 0))], axis=1))
           for m, x, a_ in zip(lt, vres, at_prev)]
    for i in range(len(units)):
        g_cat = g_cats[i]
        sv_ref[SV_RH, i] = rt_prevs[i] + rows(res[i], 0)[:, 0:QUAD]
        sv_ref[SV_YH, i] = rows(res[i], 0)[:, QUAD:] + rows(vres[i], 1)
        sv_ref[SV_BW, i] = g_cat * rows(res[i], 1)[:, 0:QUAD]
        sv_ref[SV_NT, i] = g_cat * (rows(res[i], 1)[:, QUAD:] + rows(vres[i], 2))
        sv_ref[SV_GC, i] = g_cat

    @pl.when(step < n_tiles)
    def _():
        yc_ref[0] = yc


def _mix_call(x, mod, win, cw, mu, pv, ww, wa, g2, bd, lnx, *, tt):
    bsz, t, d = x.shape
    ptot = win.shape[1]
    conv_w = cw.shape[1]
    rw = pv.shape[1]
    per_seq = t // tt
    n_tiles = bsz * per_seq
    assert tt // CHUNK <= 8, "gc scratch holds one row per chunk"

    def tile_at(lag):
        def index_map(i):
            j = jnp.clip(i - lag, 0, n_tiles - 1)
            return (j // per_seq, j % per_seq, 0)
        return index_map

    full = lambda a: pl.BlockSpec(a.shape, lambda i: (0,) * a.ndim)
    cur = lambda w: pl.BlockSpec((1, tt, w), tile_at(0))
    lag2 = lambda w: pl.BlockSpec((1, tt, w), tile_at(2))
    n_units = (tt // CHUNK) * (rw // QUAD)
    return pl.pallas_call(
        functools.partial(_mix_kernel, conv_w=conv_w, rw=rw, n_tiles=n_tiles, tiles_per_seq=per_seq),
        grid=(n_tiles + 2,),
        in_specs=[cur(d),
                  pl.BlockSpec((1,) + mod.shape[1:], lambda i: (jnp.minimum(i, n_tiles - 1) // per_seq, 0, 0)),
                  full(win), full(cw), full(mu), full(pv), full(ww), full(wa), full(g2), full(bd),
                  full(lnx)],
        out_specs=[cur(conv_w), lag2(rw)],
        out_shape=[jax.ShapeDtypeStruct((bsz, t, conv_w), BF16), jax.ShapeDtypeStruct((bsz, t, rw), BF16)],
        scratch_shapes=[pltpu.VMEM((8, conv_w), F32),
                        pltpu.VMEM((8, ptot - 3 * conv_w), F32),
                        pltpu.VMEM((rw // QUAD, HEAD, QUAD), F32),
                        pltpu.VMEM((tt, rw), F32),
                        pltpu.VMEM((5, n_units, CHUNK, QUAD), F32),
                        pltpu.VMEM((3, 2, tt, rw), F32),
                        pltpu.VMEM((2, tt, rw), BF16),
                        pltpu.VMEM((tt, rw), F32),
                        pltpu.VMEM((8, rw), F32),
                        pltpu.VMEM((n_units, 2, QUAD, QUAD), F32),
                        pltpu.VMEM((n_units, 2, CHUNK, QUAD), F32)],
        compiler_params=pltpu.CompilerParams(dimension_semantics=("arbitrary",),
                                             vmem_limit_bytes=VMEM_LIMIT),
        name="mix",
    )(x, mod, win, cw, mu, pv, ww, wa, g2, bd, lnx)


def _outmlp_kernel(x_ref, yc_ref, yr_ref, mod_ref, wo_ref, wup_ref, wdn_ref, fg_ref, o_ref,
                   *, ff_block, n_split):
    cw = yc_ref.shape[2]
    sub = x_ref.shape[1] // n_split
    x1s, hs = [], []
    for g in range(n_split):
        rs = slice(g * sub, (g + 1) * sub)
        y = (jnp.dot(yc_ref[0, rs, :], wo_ref[0:cw, :], preferred_element_type=F32)
             + jnp.dot(yr_ref[0, rs, :], wo_ref[cw:, :], preferred_element_type=F32))
        x1 = x_ref[0, rs, :] + mod_ref[0, 2:3, :] * y
        x1s.append(x1)
        hs.append((_rms(x1) * (1.0 + mod_ref[0, 4:5, :]) + mod_ref[0, 3:4, :]).astype(BF16))
    for g in range(n_split):
        f = jnp.zeros_like(x1s[g])
        for j in range(wup_ref.shape[1] // ff_block):
            hj = jnp.dot(hs[g], wup_ref[:, j * ff_block:(j + 1) * ff_block], preferred_element_type=F32)
            hj = jnp.square(jnp.maximum(hj, 0.0)).astype(BF16)
            f = f + jnp.dot(hj, wdn_ref[j * ff_block:(j + 1) * ff_block, :], preferred_element_type=F32)
        x2 = x1s[g] + mod_ref[0, 5:6, :] * f
        o_ref[0, g * sub:(g + 1) * sub, :] = _rms(x2) * fg_ref[...]


def _outmlp_call(x, yc, yr, mod, wo, wup, wdn, fg, *, tt):
    bsz, t, d = x.shape
    tok = lambda w: pl.BlockSpec((1, tt, w), lambda b, i: (b, i, 0))
    res = lambda a: pl.BlockSpec(a.shape, lambda b, i: (0,) * a.ndim, pipeline_mode=pl.Buffered(1))
    return pl.pallas_call(
        functools.partial(_outmlp_kernel, ff_block=1024, n_split=4),
        grid=(bsz, t // tt),
        in_specs=[tok(d), tok(yc.shape[2]), tok(yr.shape[2]),
                  pl.BlockSpec((1,) + mod.shape[1:], lambda b, i: (b, 0, 0)),
                  res(wo), res(wup), res(wdn), res(fg)],
        out_specs=tok(d),
        out_shape=jax.ShapeDtypeStruct((bsz, t, d), F32),
        compiler_params=pltpu.CompilerParams(dimension_semantics=("arbitrary", "arbitrary"),
                                             vmem_limit_bytes=VMEM_LIMIT),
        name="outmlp",
    )(x, yc, yr, mod, wo, wup, wdn, fg)


def kernel(x, c, w_ada, b_ada, w_in, conv_w, rwkv_mu, w0, w2, a0, a2, g2, k_k, k_a, r_k,
           lnx_g, lnx_b, w_out, w_up, w_down, final_g):
    assert w_ada.shape[0] == 1, "single-layer block"
    bsz, t, d = x.shape
    rw = w0.shape[1]
    n_mod = w_ada.shape[2] // d

    mod = _mod_call(c, w_ada[0], b_ada).reshape(bsz, n_mod, d)

    zpad = jnp.zeros((PAIR - w2.shape[1], rw), F32)
    ww = jnp.concatenate([w2[0], zpad], axis=0).astype(BF16)
    wa = jnp.concatenate([zpad, a2[0]], axis=0).astype(BF16)
    pv = jnp.concatenate([w0, a0, k_k, k_a, r_k.reshape(1, rw), jnp.zeros((3, rw), F32)], axis=0)
    hid = jnp.arange(QUAD, dtype=jnp.int32) // HEAD
    bd = (hid[:, None] == hid[None, :]).astype(BF16)
    lnx = jnp.concatenate([lnx_g, lnx_b], axis=0)

    yc, yr = _mix_call(x, mod, w_in[0].astype(BF16), conv_w[0, :, 0, :], rwkv_mu, pv, ww, wa,
                       g2[0].astype(BF16), bd, lnx, tt=512)
    return _outmlp_call(x, yc, yr, mod, w_out[0].astype(BF16), w_up[0].astype(BF16),
                        w_down[0].astype(BF16), final_g.reshape(1, d), tt=1024)
```

```python
import functools

import jax
import jax.numpy as jnp
from jax import lax
from jax.experimental import pallas as pl
from jax.experimental.pallas import tpu as pltpu

F32 = jnp.float32
BF16 = jnp.bfloat16

HEAD = 64
CHUNK = 64
PAIR = 2 * HEAD
QUAD = 4 * HEAD
EPS = 1e-6
LNX_EPS = 1e-5 * HEAD
NEG_EXP_M05 = -0.6065306597126334
VMEM_LIMIT = 56 * 1024 * 1024

MOD_BLOCK = 1024
MIX_TILE = 512
OUT_TILE = 1024
OUT_ROW_GROUPS = 4
FF_BLOCK = 1024


def _dot(a, b):
    return jnp.dot(a.astype(BF16), b.astype(BF16), preferred_element_type=F32)


def _dot_f32(a, b):
    return jnp.dot(a, b, precision=lax.Precision.HIGHEST, preferred_element_type=F32)


def _split2(x):
    hi = x.astype(BF16)
    lo = (x - hi.astype(F32)).astype(BF16)
    return hi, lo


def _quads_to_rows(x):
    return jnp.concatenate([x[:, i:i + QUAD] for i in range(0, x.shape[1], QUAD)], axis=0)


def _rows_to_quads(x, t):
    return jnp.concatenate([x[i:i + t] for i in range(0, x.shape[0], t)], axis=1)


def _headsum1(x, bd):
    s = jnp.dot(_quads_to_rows(x.astype(BF16)), bd, preferred_element_type=F32)
    return _rows_to_quads(s, x.shape[0])


def _rms(x):
    return x * lax.rsqrt(jnp.mean(x * x, axis=-1, keepdims=True) + EPS)


def _mod_kernel(c_ref, w_ref, b_ref, o_ref):
    c = c_ref[...]
    cond = c * jax.nn.sigmoid(c)
    o_ref[...] = _dot_f32(cond, w_ref[...]) + b_ref[...]


def _mod_call(c, w_ada, b_ada):
    bsz, d = c.shape
    n = w_ada.shape[1]
    bn = MOD_BLOCK
    return pl.pallas_call(
        _mod_kernel,
        grid=(n // bn,),
        in_specs=[pl.BlockSpec((bsz, d), lambda j: (0, 0)),
                  pl.BlockSpec((d, bn), lambda j: (0, j)),
                  pl.BlockSpec((1, bn), lambda j: (0, j))],
        out_specs=pl.BlockSpec((bsz, bn), lambda j: (0, j)),
        out_shape=jax.ShapeDtypeStruct((bsz, n), F32),
        compiler_params=pltpu.CompilerParams(dimension_semantics=("arbitrary",),
                                             vmem_limit_bytes=VMEM_LIMIT),
        name="mod",
    )(c, w_ada, b_ada)


SV_RH, SV_YH, SV_BW, SV_NT, SV_GC = range(5)
AB_AT, AB_V = range(2)
GT_BT, GT_KT = range(2)


def _mix_kernel(x_ref, mod_ref, win_ref, cw_ref, mu_ref, pv_ref, ww_ref, wa_ref, g2_ref, bd_ref, lnx_ref,
                yc_ref, yr_ref,
                zc_ref, qc_ref, p_scr, y_scr, sv_ref, bg_ref, ab_ref, rt_ref, gc_ref, gt_ref, tc_ref,
                *, conv_w, rw, n_tiles, tiles_per_seq):
    tt = x_ref.shape[1]
    step = pl.program_id(0)
    seq_start = lax.rem(step, jnp.int32(tiles_per_seq)) == 0
    chain_seq_start = lax.rem(step - 2, jnp.int32(tiles_per_seq)) == 0
    nquad = rw // QUAD
    nheads = QUAD // HEAD
    nchunk = tt // CHUNK

    @pl.when(step == 0)
    def _():
        for ref in (zc_ref, qc_ref, p_scr, sv_ref, bg_ref, ab_ref, rt_ref, gc_ref, gt_ref, tc_ref):
            ref[...] = jnp.zeros_like(ref)

    bd = bd_ref[...]
    ti = lax.broadcasted_iota(jnp.int32, (CHUNK, QUAD), 0)
    li = lax.broadcasted_iota(jnp.int32, (CHUNK, QUAD), 1)
    strict = ti > (li % HEAD)
    incl = ti >= (li % HEAD)
    eye_cat = jnp.where(ti == (li % HEAD), 1.0, 0.0).astype(F32)
    head_of_lane = li // HEAD
    qi = lax.broadcasted_iota(jnp.int32, (QUAD, QUAD), 0)
    qj = lax.broadcasted_iota(jnp.int32, (QUAD, QUAD), 1)
    eye_bd = qi == qj

    def bd_of(m):
        mb = m.astype(BF16)
        zero = jnp.zeros_like(mb)
        return jnp.concatenate([jnp.where(head_of_lane == hh, mb, zero) for hh in range(nheads)], axis=0)

    def col_to_cat(col):
        out = col[0:HEAD]
        for hh in range(1, nheads):
            out = jnp.where(head_of_lane == hh, col[hh * HEAD:(hh + 1) * HEAD], out)
        return out

    units = [(c, qd) for c in range(nchunk) for qd in range(nquad)]

    def rows(m, i):
        return m[i * CHUNK:(i + 1) * CHUNK]

    def ab(kind, u):
        c, qd = u
        return ab_ref[kind, c * CHUNK:(c + 1) * CHUNK, qd * QUAD:(qd + 1) * QUAD]

    def rt_prev(u):
        c, qd = u
        return rt_ref[c * CHUNK:(c + 1) * CHUNK, qd * QUAD:(qd + 1) * QUAD]

    state = [jnp.where(chain_seq_start, 0.0, p_scr[qd]) for qd in range(nquad)]
    pending = list(range(nchunk))

    def chain_step():
        if not pending:
            return
        c = pending.pop(0)
        for qd in range(nquad):
            i = c * nquad + qd
            pp = state[qd]
            m = _dot(jnp.concatenate([sv_ref[SV_RH, i], sv_ref[SV_BW, i]], axis=0), bd_of(pp))
            y_scr[c * CHUNK:(c + 1) * CHUNK, qd * QUAD:(qd + 1) * QUAD] = rows(m, 0) + sv_ref[SV_YH, i]
            state[qd] = sv_ref[SV_GC, i] * pp + rows(m, 1) + sv_ref[SV_NT, i]

    x = x_ref[0]
    h = _rms(x) * (1.0 + mod_ref[0, 1:2, :]) + mod_ref[0, 0:1, :]
    p = jnp.dot(h.astype(BF16), win_ref[...], preferred_element_type=F32)
    first = lax.broadcasted_iota(jnp.int32, (8, 1), 0) == 0
    cw = conv_w

    def shift_rows(m, carry_row):
        rolled = pltpu.roll(m, 1, 0)
        return jnp.concatenate([jnp.where(first, carry_row, rolled[0:8]), rolled[8:]], axis=0)

    at_prev = [ab(AB_AT, u) for u in units]
    v_prev = [ab(AB_V, u) for u in units]
    rt_prevs = [rt_prev(u) for u in units]
    g_cats = [col_to_cat(jnp.sum(jnp.where(eye_bd, gc_ref[c:c + 1, qd * QUAD:(qd + 1) * QUAD], 0.0),
                                 axis=1, keepdims=True)) for c, qd in units]
    gram = [_dot(jnp.concatenate([at_prev[i], rt_prevs[i].astype(BF16)], axis=0),
                 jnp.concatenate([gt_ref[i, GT_BT], gt_ref[i, GT_KT]], axis=1))
            for i in range(len(units))]
    chain_step()

    z = p[:, cw:2 * cw] * p[:, 2 * cw:3 * cw]
    z1 = shift_rows(z, jnp.where(seq_start, 0.0, zc_ref[0:1, :]))
    z2 = shift_rows(z1, jnp.where(seq_start, 0.0, zc_ref[1:2, :]))
    zc_ref[0:1, :] = z[tt - 1:tt, :]
    zc_ref[1:2, :] = z1[tt - 1:tt, :]
    yc = (p[:, 0:cw] * (cw_ref[0:1, :] * z2 + cw_ref[1:2, :] * z1 + cw_ref[2:3, :] * z)).astype(BF16)

    aab = [jnp.where(strict, rows(m, 0)[:, 0:QUAD], 0.0) for m in gram]
    aak = [jnp.where(strict, rows(m, 0)[:, QUAD:], 0.0) for m in gram]
    rb = [jnp.where(incl, rows(m, 1)[:, 0:QUAD], 0.0) for m in gram]
    rk = [jnp.where(incl, rows(m, 1)[:, QUAD:], 0.0) for m in gram]
    btt = [tc_ref[i, GT_BT].astype(BF16) for i in range(len(units))]
    ktt = [tc_ref[i, GT_KT].astype(BF16) for i in range(len(units))]
    tinv = [eye_cat + m for m in aab]
    apow = [_dot(m, bd_of(m)) for m in aab]

    q = p[:, 3 * cw:]
    qprev = shift_rows(q, jnp.where(seq_start, 0.0, qc_ref[0:1, :]))
    qc_ref[0:1, :] = q[tt - 1:tt, :]
    q = q + (qprev - q) * mu_ref[...]
    r = q[:, 0:rw]
    k = q[:, rw:2 * rw]
    v_new = q[:, 2 * rw:3 * rw]
    wa = q[:, 3 * rw:3 * rw + PAIR]
    gl = q[:, 3 * rw + PAIR:]
    lane = lax.broadcasted_iota(jnp.int32, (1, PAIR), 1)
    wa = jnp.where(lane < HEAD, jnp.tanh(wa), wa)
    w_pre = _dot(wa, ww_ref[...])
    a_pre = _dot(wa, wa_ref[...])
    gate = _dot(jax.nn.sigmoid(gl), g2_ref[...])
    w0, a0, k_k, k_a, r_k = (pv_ref[i:i + 1, :] for i in range(5))

    def level(apow, tinv):
        sq = [_dot(jnp.concatenate([m, t], axis=0), bd_of(m)) for m, t in zip(apow, tinv)]
        return [rows(m, 0) for m in sq], [t + rows(m, 1) for t, m in zip(tinv, sq)]

    n_levels = 0
    while (4 << n_levels) < CHUNK:
        n_levels += 1
    for _ in range(n_levels // 2):
        apow, tinv = level(apow, tinv)
        chain_step()

    ld = jax.nn.sigmoid(w0 + w_pre) * NEG_EXP_M05
    a = jax.nn.sigmoid(a0 + a_pre)
    kk = k * k_k
    kk = kk / jnp.maximum(jnp.sqrt(_headsum1(kk * kk, bd)), 1e-12)
    k2 = k * (1.0 + (a - 1.0) * k_a)
    b = kk * a
    bonus = _headsum1(r * k2 * r_k, bd) * v_new

    for _ in range(n_levels - n_levels // 2):
        apow, tinv = level(apow, tinv)
        chain_step()
    tinv = [t + _dot(t, bd_of(m)) for m, t in zip(apow, tinv)]

    ri = lax.broadcasted_iota(jnp.int32, (tt, tt), 0)
    ci = lax.broadcasted_iota(jnp.int32, (tt, tt), 1)
    tril = jnp.where(((ri // CHUNK) == (ci // CHUNK)) & (ri >= ci), 1.0, 0.0).astype(BF16)
    cum = sum(jnp.dot(tril, part, preferred_element_type=F32) for part in _split2(ld))
    last = [cum[c * CHUNK + CHUNK - 1:(c + 1) * CHUNK, :] for c in range(nchunk)]
    einv = jnp.exp(-cum)

    vres = [_dot(jnp.concatenate([m1.astype(BF16), m2.astype(BF16), m3], axis=0), bd_of(v))
            for m1, m2, m3, v in zip(aak, rk, ktt, v_prev)]
    while pending:
        chain_step()

    bt_new = b * einv
    kt_new = k2 * einv
    rt_ref[...] = r * jnp.exp(cum)
    ab_ref[AB_AT] = (-kk * jnp.exp(cum - ld)).astype(BF16)
    ab_ref[AB_V] = v_new.astype(BF16)
    for c in range(nchunk):
        gc_ref[c:c + 1, :] = jnp.exp(last[c])
    slot_in = lax.rem(step, jnp.int32(3))
    bg_ref[slot_in, 0] = bonus
    bg_ref[slot_in, 1] = gate
    for i, (c, qd) in enumerate(units):
        for kind, full_tile in ((GT_BT, bt_new), (GT_KT, kt_new)):
            xt = jnp.transpose(full_tile[c * CHUNK:(c + 1) * CHUNK, qd * QUAD:(qd + 1) * QUAD])
            for hh in range(nheads):
                blk_t = xt[hh * HEAD:(hh + 1) * HEAD, :]
                gt_ref[i, kind, hh * HEAD:(hh + 1) * HEAD, hh * CHUNK:(hh + 1) * CHUNK] = blk_t
                tc_ref[i, kind, :, hh * CHUNK:(hh + 1) * CHUNK] = blk_t

    lt = [_dot(jnp.concatenate([m, bt], axis=0), bd_of(t)) for m, bt, t in zip(rb, btt, tinv)]

    for qd in range(nquad):
        p_scr[qd] = state[qd]
    slot_out = lax.rem(step + 1, jnp.int32(3))
    y = y_scr[...]
    d = y - _headsum1(y, bd) * (1.0 / HEAD)
    var = _headsum1(d * d, bd) * (1.0 / HEAD)
    yn = d * lax.rsqrt(var + LNX_EPS)
    yn = yn * lnx_ref[0:1, :] + lnx_ref[1:2, :] + bg_ref[slot_out, 0]
    yr_ref[0] = (yn * bg_ref[slot_out, 1]).astype(BF16)

    res = [_dot(m, jnp.concatenate([bd_of(a_), bd_of(rows(x, 0))], axis=1))
           for m, x, a_ in zip(lt, vres, at_prev)]
    for i in range(len(units)):
        g_cat = g_cats[i]
        sv_ref[SV_RH, i] = rt_prevs[i] + rows(res[i], 0)[:, 0:QUAD]
        sv_ref[SV_YH, i] = rows(res[i], 0)[:, QUAD:] + rows(vres[i], 1)
        sv_ref[SV_BW, i] = g_cat * rows(res[i], 1)[:, 0:QUAD]
        sv_ref[SV_NT, i] = g_cat * (rows(res[i], 1)[:, QUAD:] + rows(vres[i], 2))
        sv_ref[SV_GC, i] = g_cat

    @pl.when(step < n_tiles)
    def _():
        yc_ref[0] = yc


def _mix_call(x, mod, win, cw, mu, pv, ww, wa, g2, bd, lnx, *, tt):
    bsz, t, d = x.shape
    ptot = win.shape[1]
    conv_w = cw.shape[1]
    rw = pv.shape[1]
    per_seq = t // tt
    n_tiles = bsz * per_seq
    assert tt // CHUNK <= 8, "gc scratch holds one row per chunk"

    def tile_at(lag):
        def index_map(i):
            j = jnp.clip(i - lag, 0, n_tiles - 1)
            return (j // per_seq, j % per_seq, 0)
        return index_map

    full = lambda a: pl.BlockSpec(a.shape, lambda i: (0,) * a.ndim)
    cur = lambda w: pl.BlockSpec((1, tt, w), tile_at(0))
    lag2 = lambda w: pl.BlockSpec((1, tt, w), tile_at(2))
    n_units = (tt // CHUNK) * (rw // QUAD)
    return pl.pallas_call(
        functools.partial(_mix_kernel, conv_w=conv_w, rw=rw, n_tiles=n_tiles, tiles_per_seq=per_seq),
        grid=(n_tiles + 2,),
        in_specs=[cur(d),
                  pl.BlockSpec((1,) + mod.shape[1:], lambda i: (jnp.minimum(i, n_tiles - 1) // per_seq, 0, 0)),
                  full(win), full(cw), full(mu), full(pv), full(ww), full(wa), full(g2), full(bd),
                  full(lnx)],
        out_specs=[cur(conv_w), lag2(rw)],
        out_shape=[jax.ShapeDtypeStruct((bsz, t, conv_w), BF16), jax.ShapeDtypeStruct((bsz, t, rw), BF16)],
        scratch_shapes=[pltpu.VMEM((8, conv_w), F32),
                        pltpu.VMEM((8, ptot - 3 * conv_w), F32),
                        pltpu.VMEM((rw // QUAD, HEAD, QUAD), F32),
                        pltpu.VMEM((tt, rw), F32),
                        pltpu.VMEM((5, n_units, CHUNK, QUAD), F32),
                        pltpu.VMEM((3, 2, tt, rw), F32),
                        pltpu.VMEM((2, tt, rw), BF16),
                        pltpu.VMEM((tt, rw), F32),
                        pltpu.VMEM((8, rw), F32),
                        pltpu.VMEM((n_units, 2, QUAD, QUAD), F32),
                        pltpu.VMEM((n_units, 2, CHUNK, QUAD), F32)],
        compiler_params=pltpu.CompilerParams(dimension_semantics=("arbitrary",),
                                             vmem_limit_bytes=VMEM_LIMIT),
        name="mix",
    )(x, mod, win, cw, mu, pv, ww, wa, g2, bd, lnx)


def _outmlp_kernel(x_ref, yc_ref, yr_ref, mod_ref, wo_ref, wup_ref, wdn_ref, fg_ref, o_ref,
                   *, ff_block, n_split):
    cw = yc_ref.shape[2]
    sub = x_ref.shape[1] // n_split
    x1s, hs = [], []
    for g in range(n_split):
        rs = slice(g * sub, (g + 1) * sub)
        y = (jnp.dot(yc_ref[0, rs, :], wo_ref[0:cw, :], preferred_element_type=F32)
             + jnp.dot(yr_ref[0, rs, :], wo_ref[cw:, :], preferred_element_type=F32))
        x1 = x_ref[0, rs, :] + mod_ref[0, 2:3, :] * y
        x1s.append(x1)
        hs.append((_rms(x1) * (1.0 + mod_ref[0, 4:5, :]) + mod_ref[0, 3:4, :]).astype(BF16))
    for g in range(n_split):
        f = jnp.zeros_like(x1s[g])
        for j in range(wup_ref.shape[1] // ff_block):
            hj = jnp.dot(hs[g], wup_ref[:, j * ff_block:(j + 1) * ff_block], preferred_element_type=F32)
            hj = jnp.square(jnp.maximum(hj, 0.0)).astype(BF16)
            f = f + jnp.dot(hj, wdn_ref[j * ff_block:(j + 1) * ff_block, :], preferred_element_type=F32)
        x2 = x1s[g] + mod_ref[0, 5:6, :] * f
        o_ref[0, g * sub:(g + 1) * sub, :] = _rms(x2) * fg_ref[...]


def _outmlp_call(x, yc, yr, mod, wo, wup, wdn, fg, *, tt):
    bsz, t, d = x.shape
    tok = lambda w: pl.BlockSpec((1, tt, w), lambda b, i: (b, i, 0))
    res = lambda a: pl.BlockSpec(a.shape, lambda b, i: (0,) * a.ndim, pipeline_mode=pl.Buffered(1))
    return pl.pallas_call(
        functools.partial(_outmlp_kernel, ff_block=FF_BLOCK, n_split=OUT_ROW_GROUPS),
        grid=(bsz, t // tt),
        in_specs=[tok(d), tok(yc.shape[2]), tok(yr.shape[2]),
                  pl.BlockSpec((1,) + mod.shape[1:], lambda b, i: (b, 0, 0)),
                  res(wo), res(wup), res(wdn), res(fg)],
        out_specs=tok(d),
        out_shape=jax.ShapeDtypeStruct((bsz, t, d), F32),
        compiler_params=pltpu.CompilerParams(dimension_semantics=("arbitrary", "arbitrary"),
                                             vmem_limit_bytes=VMEM_LIMIT),
        name="outmlp",
    )(x, yc, yr, mod, wo, wup, wdn, fg)


def kernel(x, c, w_ada, b_ada, w_in, conv_w, rwkv_mu, w0, w2, a0, a2, g2, k_k, k_a, r_k,
           lnx_g, lnx_b, w_out, w_up, w_down, final_g):
    assert w_ada.shape[0] == 1, "single-layer block"
    bsz, t, d = x.shape
    rw = w0.shape[1]
    n_mod = w_ada.shape[2] // d
    assert t % MIX_TILE == 0 and t % OUT_TILE == 0 and MIX_TILE % CHUNK == 0 and OUT_TILE % OUT_ROW_GROUPS == 0
    assert rw % QUAD == 0 and r_k.shape[-1] == HEAD and w_ada.shape[2] % MOD_BLOCK == 0
    assert w2.shape[1] + a2.shape[1] == PAIR and w_up.shape[2] % FF_BLOCK == 0

    mod = _mod_call(c, w_ada[0], b_ada).reshape(bsz, n_mod, d)

    zpad = jnp.zeros((PAIR - w2.shape[1], rw), F32)
    ww = jnp.concatenate([w2[0], zpad], axis=0).astype(BF16)
    wa = jnp.concatenate([zpad, a2[0]], axis=0).astype(BF16)
    pv = jnp.concatenate([w0, a0, k_k, k_a, r_k.reshape(1, rw), jnp.zeros((3, rw), F32)], axis=0)
    hid = jnp.arange(QUAD, dtype=jnp.int32) // HEAD
    bd = (hid[:, None] == hid[None, :]).astype(BF16)
    lnx = jnp.concatenate([lnx_g, lnx_b], axis=0)

    yc, yr = _mix_call(x, mod, w_in[0].astype(BF16), conv_w[0, :, 0, :], rwkv_mu, pv, ww, wa,
                       g2[0].astype(BF16), bd, lnx, tt=MIX_TILE)
    return _outmlp_call(x, yc, yr, mod, w_out[0].astype(BF16), w_up[0].astype(BF16),
                        w_down[0].astype(BF16), final_g.reshape(1, d), tt=OUT_TILE)
```

```python
import functools

import jax
import jax.numpy as jnp
from jax import lax
from jax.experimental import pallas as pl
from jax.experimental.pallas import tpu as pltpu

F32 = jnp.float32
BF16 = jnp.bfloat16

HEAD = 64
CHUNK = 64
PAIR = 2 * HEAD
QUAD = 4 * HEAD
EPS = 1e-6
LNX_EPS = 1e-5 * HEAD
NEG_EXP_M05 = -0.6065306597126334
VMEM_LIMIT = 56 * 1024 * 1024

MOD_BLOCK = 1024
MIX_TILE = 512
OUT_TILE = 1024
OUT_ROW_GROUPS = 4
FF_BLOCK = 1024


def _dot(a, b):
    return jnp.dot(a.astype(BF16), b.astype(BF16), preferred_element_type=F32)


def _dot_f32(a, b):
    return jnp.dot(a, b, precision=lax.Precision.HIGHEST, preferred_element_type=F32)


def _split2(x):
    hi = x.astype(BF16)
    lo = (x - hi.astype(F32)).astype(BF16)
    return hi, lo


def _quads_to_rows(x):
    return jnp.concatenate([x[:, i:i + QUAD] for i in range(0, x.shape[1], QUAD)], axis=0)


def _rows_to_quads(x, t):
    return jnp.concatenate([x[i:i + t] for i in range(0, x.shape[0], t)], axis=1)


def _headsum1(x, bd):
    s = jnp.dot(_quads_to_rows(x.astype(BF16)), bd, preferred_element_type=F32)
    return _rows_to_quads(s, x.shape[0])


def _rms(x):
    return x * lax.rsqrt(jnp.mean(x * x, axis=-1, keepdims=True) + EPS)


def _mod_kernel(c_ref, w_ref, b_ref, o_ref):
    c = c_ref[...]
    cond = c * jax.nn.sigmoid(c)
    o_ref[...] = _dot_f32(cond, w_ref[...]) + b_ref[...]


def _mod_call(c, w_ada, b_ada):
    bsz, d = c.shape
    n = w_ada.shape[1]
    bn = MOD_BLOCK
    return pl.pallas_call(
        _mod_kernel,
        grid=(n // bn,),
        in_specs=[pl.BlockSpec((bsz, d), lambda j: (0, 0)),
                  pl.BlockSpec((d, bn), lambda j: (0, j)),
                  pl.BlockSpec((1, bn), lambda j: (0, j))],
        out_specs=pl.BlockSpec((bsz, bn), lambda j: (0, j)),
        out_shape=jax.ShapeDtypeStruct((bsz, n), F32),
        compiler_params=pltpu.CompilerParams(dimension_semantics=("arbitrary",),
                                             vmem_limit_bytes=VMEM_LIMIT),
        name="mod",
    )(c, w_ada, b_ada)


SV_RH, SV_YH, SV_BW, SV_NT, SV_GC = range(5)
AB_AT, AB_V = range(2)
GT_BT, GT_KT = range(2)


def _mix_kernel(x_ref, mod_ref, win_ref, cw_ref, mu_ref, pv_ref, ww_ref, wa_ref, g2_ref, bd_ref, lnx_ref,
                yc_ref, yr_ref,
                zc_ref, qc_ref, p_scr, y_scr, sv_ref, bg_ref, ab_ref, rt_ref, gc_ref, gt_ref,
                *, conv_w, rw, n_tiles, tiles_per_seq):
    tt = x_ref.shape[1]
    step = pl.program_id(0)
    seq_start = lax.rem(step, jnp.int32(tiles_per_seq)) == 0
    chain_seq_start = lax.rem(step - 2, jnp.int32(tiles_per_seq)) == 0
    nquad = rw // QUAD
    nheads = QUAD // HEAD
    nchunk = tt // CHUNK

    @pl.when(step == 0)
    def _():
        for ref in (zc_ref, qc_ref, p_scr, sv_ref, bg_ref, ab_ref, rt_ref, gc_ref, gt_ref):
            ref[...] = jnp.zeros_like(ref)

    bd = bd_ref[...]
    ti = lax.broadcasted_iota(jnp.int32, (CHUNK, QUAD), 0)
    li = lax.broadcasted_iota(jnp.int32, (CHUNK, QUAD), 1)
    strict = ti > (li % HEAD)
    incl = ti >= (li % HEAD)
    eye_cat = jnp.where(ti == (li % HEAD), 1.0, 0.0).astype(F32)
    head_of_lane = li // HEAD
    qi = lax.broadcasted_iota(jnp.int32, (QUAD, QUAD), 0)
    qj = lax.broadcasted_iota(jnp.int32, (QUAD, QUAD), 1)
    eye_bd = qi == qj

    def bd_of(m):
        mb = m.astype(BF16)
        zero = jnp.zeros_like(mb)
        return jnp.concatenate([jnp.where(head_of_lane == hh, mb, zero) for hh in range(nheads)], axis=0)

    def col_to_cat(col):
        out = col[0:HEAD]
        for hh in range(1, nheads):
            out = jnp.where(head_of_lane == hh, col[hh * HEAD:(hh + 1) * HEAD], out)
        return out

    units = [(c, qd) for c in range(nchunk) for qd in range(nquad)]

    def rows(m, i):
        return m[i * CHUNK:(i + 1) * CHUNK]

    def ab(kind, u):
        c, qd = u
        return ab_ref[kind, c * CHUNK:(c + 1) * CHUNK, qd * QUAD:(qd + 1) * QUAD]

    def rt_prev(u):
        c, qd = u
        return rt_ref[c * CHUNK:(c + 1) * CHUNK, qd * QUAD:(qd + 1) * QUAD]

    state = [jnp.where(chain_seq_start, 0.0, p_scr[qd]) for qd in range(nquad)]
    pending = list(range(nchunk))

    def chain_step():
        if not pending:
            return
        c = pending.pop(0)
        for qd in range(nquad):
            i = c * nquad + qd
            pp = state[qd]
            m = _dot(jnp.concatenate([sv_ref[SV_RH, i], sv_ref[SV_BW, i]], axis=0), bd_of(pp))
            y_scr[c * CHUNK:(c + 1) * CHUNK, qd * QUAD:(qd + 1) * QUAD] = rows(m, 0) + sv_ref[SV_YH, i]
            state[qd] = sv_ref[SV_GC, i] * pp + rows(m, 1) + sv_ref[SV_NT, i]

    x = x_ref[0]
    h = _rms(x) * (1.0 + mod_ref[0, 1:2, :]) + mod_ref[0, 0:1, :]
    p = jnp.dot(h.astype(BF16), win_ref[...], preferred_element_type=F32)
    first = lax.broadcasted_iota(jnp.int32, (8, 1), 0) == 0
    cw = conv_w

    def shift_rows(m, carry_row):
        rolled = pltpu.roll(m, 1, 0)
        return jnp.concatenate([jnp.where(first, carry_row, rolled[0:8]), rolled[8:]], axis=0)

    at_prev = [ab(AB_AT, u) for u in units]
    v_prev = [ab(AB_V, u) for u in units]
    rt_prevs = [rt_prev(u) for u in units]
    g_cats = [col_to_cat(jnp.sum(jnp.where(eye_bd, gc_ref[c:c + 1, qd * QUAD:(qd + 1) * QUAD], 0.0),
                                 axis=1, keepdims=True)) for c, qd in units]
    gts = [jnp.concatenate([gt_ref[i, GT_BT], gt_ref[i, GT_KT]], axis=1) for i in range(len(units))]
    gram = [_dot(jnp.concatenate([at_prev[i], rt_prevs[i].astype(BF16)], axis=0), gts[i])
            for i in range(len(units))]
    tcs = [sum(rows(g, hh) for hh in range(nheads)).astype(BF16) for g in gts]
    chain_step()

    z = p[:, cw:2 * cw] * p[:, 2 * cw:3 * cw]
    z1 = shift_rows(z, jnp.where(seq_start, 0.0, zc_ref[0:1, :]))
    z2 = shift_rows(z1, jnp.where(seq_start, 0.0, zc_ref[1:2, :]))
    zc_ref[0:1, :] = z[tt - 1:tt, :]
    zc_ref[1:2, :] = z1[tt - 1:tt, :]
    yc = (p[:, 0:cw] * (cw_ref[0:1, :] * z2 + cw_ref[1:2, :] * z1 + cw_ref[2:3, :] * z)).astype(BF16)

    aab = [jnp.where(strict, rows(m, 0)[:, 0:QUAD], 0.0) for m in gram]
    aak = [jnp.where(strict, rows(m, 0)[:, QUAD:], 0.0) for m in gram]
    rb = [jnp.where(incl, rows(m, 1)[:, 0:QUAD], 0.0) for m in gram]
    rk = [jnp.where(incl, rows(m, 1)[:, QUAD:], 0.0) for m in gram]
    btt = [m[:, 0:QUAD] for m in tcs]
    ktt = [m[:, QUAD:] for m in tcs]
    tinv = [eye_cat + m for m in aab]
    apow = [_dot(m, bd_of(m)) for m in aab]

    q = p[:, 3 * cw:]
    qprev = shift_rows(q, jnp.where(seq_start, 0.0, qc_ref[0:1, :]))
    qc_ref[0:1, :] = q[tt - 1:tt, :]
    q = q + (qprev - q) * mu_ref[...]
    r = q[:, 0:rw]
    k = q[:, rw:2 * rw]
    v_new = q[:, 2 * rw:3 * rw]
    wa = q[:, 3 * rw:3 * rw + PAIR]
    gl = q[:, 3 * rw + PAIR:]
    lane = lax.broadcasted_iota(jnp.int32, (1, PAIR), 1)
    wa = jnp.where(lane < HEAD, jnp.tanh(wa), wa)
    w_pre = _dot(wa, ww_ref[...])
    a_pre = _dot(wa, wa_ref[...])
    gate = _dot(jax.nn.sigmoid(gl), g2_ref[...])
    w0, a0, k_k, k_a, r_k = (pv_ref[i:i + 1, :] for i in range(5))

    def level(apow, tinv):
        sq = [_dot(jnp.concatenate([m, t], axis=0), bd_of(m)) for m, t in zip(apow, tinv)]
        return [rows(m, 0) for m in sq], [t + rows(m, 1) for t, m in zip(tinv, sq)]

    n_levels = 0
    while (4 << n_levels) < CHUNK:
        n_levels += 1
    for _ in range(n_levels // 2):
        apow, tinv = level(apow, tinv)
        chain_step()

    ld = jax.nn.sigmoid(w0 + w_pre) * NEG_EXP_M05
    a = jax.nn.sigmoid(a0 + a_pre)
    kk = k * k_k
    kk = kk / jnp.maximum(jnp.sqrt(_headsum1(kk * kk, bd)), 1e-12)
    k2 = k * (1.0 + (a - 1.0) * k_a)
    b = kk * a
    bonus = _headsum1(r * k2 * r_k, bd) * v_new

    for _ in range(n_levels - n_levels // 2):
        apow, tinv = level(apow, tinv)
        chain_step()
    tinv = [t + _dot(t, bd_of(m)) for m, t in zip(apow, tinv)]

    ri = lax.broadcasted_iota(jnp.int32, (tt, tt), 0)
    ci = lax.broadcasted_iota(jnp.int32, (tt, tt), 1)
    tril = jnp.where(((ri // CHUNK) == (ci // CHUNK)) & (ri >= ci), 1.0, 0.0).astype(BF16)
    cum = sum(jnp.dot(tril, part, preferred_element_type=F32) for part in _split2(ld))
    last = [cum[c * CHUNK + CHUNK - 1:(c + 1) * CHUNK, :] for c in range(nchunk)]
    einv = jnp.exp(-cum)

    vres = [_dot(jnp.concatenate([m1.astype(BF16), m2.astype(BF16), m3], axis=0), bd_of(v))
            for m1, m2, m3, v in zip(aak, rk, ktt, v_prev)]
    while pending:
        chain_step()

    bt_new = b * einv
    kt_new = k2 * einv
    rt_ref[...] = r * jnp.exp(cum)
    ab_ref[AB_AT] = (-kk * jnp.exp(cum - ld)).astype(BF16)
    ab_ref[AB_V] = v_new.astype(BF16)
    for c in range(nchunk):
        gc_ref[c:c + 1, :] = jnp.exp(last[c])
    slot_in = lax.rem(step, jnp.int32(3))
    bg_ref[slot_in, 0] = bonus
    bg_ref[slot_in, 1] = gate
    for i, (c, qd) in enumerate(units):
        for kind, full_tile in ((GT_BT, bt_new), (GT_KT, kt_new)):
            xt = jnp.transpose(full_tile[c * CHUNK:(c + 1) * CHUNK, qd * QUAD:(qd + 1) * QUAD])
            for hh in range(nheads):
                blk_t = xt[hh * HEAD:(hh + 1) * HEAD, :]
                gt_ref[i, kind, hh * HEAD:(hh + 1) * HEAD, hh * CHUNK:(hh + 1) * CHUNK] = blk_t

    lt = [_dot(jnp.concatenate([m, bt], axis=0), bd_of(t)) for m, bt, t in zip(rb, btt, tinv)]

    for qd in range(nquad):
        p_scr[qd] = state[qd]
    slot_out = lax.rem(step + 1, jnp.int32(3))
    y = y_scr[...]
    d = y - _headsum1(y, bd) * (1.0 / HEAD)
    var = _headsum1(d * d, bd) * (1.0 / HEAD)
    yn = d * lax.rsqrt(var + LNX_EPS)
    yn = yn * lnx_ref[0:1, :] + lnx_ref[1:2, :] + bg_ref[slot_out, 0]
    yr_ref[0] = (yn * bg_ref[slot_out, 1]).astype(BF16)

    res = [_dot(m, jnp.concatenate([bd_of(a_), bd_of(rows(x, 0))], axis=1))
           for m, x, a_ in zip(lt, vres, at_prev)]
    for i in range(len(units)):
        g_cat = g_cats[i]
        sv_ref[SV_RH, i] = rt_prevs[i] + rows(res[i], 0)[:, 0:QUAD]
        sv_ref[SV_YH, i] = rows(res[i], 0)[:, QUAD:] + rows(vres[i], 1)
        sv_ref[SV_BW, i] = g_cat * rows(res[i], 1)[:, 0:QUAD]
        sv_ref[SV_NT, i] = g_cat * (rows(res[i], 1)[:, QUAD:] + rows(vres[i], 2))
        sv_ref[SV_GC, i] = g_cat

    @pl.when(step < n_tiles)
    def _():
        yc_ref[0] = yc


def _mix_call(x, mod, win, cw, mu, pv, ww, wa, g2, bd, lnx, *, tt):
    bsz, t, d = x.shape
    ptot = win.shape[1]
    conv_w = cw.shape[1]
    rw = pv.shape[1]
    per_seq = t // tt
    n_tiles = bsz * per_seq
    assert tt // CHUNK <= 8, "gc scratch holds one row per chunk"

    def tile_at(lag):
        def index_map(i):
            j = jnp.clip(i - lag, 0, n_tiles - 1)
            return (j // per_seq, j % per_seq, 0)
        return index_map

    full = lambda a: pl.BlockSpec(a.shape, lambda i: (0,) * a.ndim)
    cur = lambda w: pl.BlockSpec((1, tt, w), tile_at(0))
    lag2 = lambda w: pl.BlockSpec((1, tt, w), tile_at(2))
    n_units = (tt // CHUNK) * (rw // QUAD)
    return pl.pallas_call(
        functools.partial(_mix_kernel, conv_w=conv_w, rw=rw, n_tiles=n_tiles, tiles_per_seq=per_seq),
        grid=(n_tiles + 2,),
        in_specs=[cur(d),
                  pl.BlockSpec((1,) + mod.shape[1:], lambda i: (jnp.minimum(i, n_tiles - 1) // per_seq, 0, 0)),
                  full(win), full(cw), full(mu), full(pv), full(ww), full(wa), full(g2), full(bd),
                  full(lnx)],
        out_specs=[cur(conv_w), lag2(rw)],
        out_shape=[jax.ShapeDtypeStruct((bsz, t, conv_w), BF16), jax.ShapeDtypeStruct((bsz, t, rw), BF16)],
        scratch_shapes=[pltpu.VMEM((8, conv_w), F32),
                        pltpu.VMEM((8, ptot - 3 * conv_w), F32),
                        pltpu.VMEM((rw // QUAD, HEAD, QUAD), F32),
                        pltpu.VMEM((tt, rw), F32),
                        pltpu.VMEM((5, n_units, CHUNK, QUAD), F32),
                        pltpu.VMEM((3, 2, tt, rw), F32),
                        pltpu.VMEM((2, tt, rw), BF16),
                        pltpu.VMEM((tt, rw), F32),
                        pltpu.VMEM((8, rw), F32),
                        pltpu.VMEM((n_units, 2, QUAD, QUAD), F32)],
        compiler_params=pltpu.CompilerParams(dimension_semantics=("arbitrary",),
                                             vmem_limit_bytes=VMEM_LIMIT),
        name="mix",
    )(x, mod, win, cw, mu, pv, ww, wa, g2, bd, lnx)


def _outmlp_kernel(x_ref, yc_ref, yr_ref, mod_ref, wo_ref, wup_ref, wdn_ref, fg_ref, o_ref,
                   *, ff_block, n_split):
    cw = yc_ref.shape[2]
    sub = x_ref.shape[1] // n_split
    x1s, hs = [], []
    for g in range(n_split):
        rs = slice(g * sub, (g + 1) * sub)
        y = (jnp.dot(yc_ref[0, rs, :], wo_ref[0:cw, :], preferred_element_type=F32)
             + jnp.dot(yr_ref[0, rs, :], wo_ref[cw:, :], preferred_element_type=F32))
        x1 = x_ref[0, rs, :] + mod_ref[0, 2:3, :] * y
        x1s.append(x1)
        hs.append((_rms(x1) * (1.0 + mod_ref[0, 4:5, :]) + mod_ref[0, 3:4, :]).astype(BF16))
    for g in range(n_split):
        f = jnp.zeros_like(x1s[g])
        for j in range(wup_ref.shape[1] // ff_block):
            hj = jnp.dot(hs[g], wup_ref[:, j * ff_block:(j + 1) * ff_block], preferred_element_type=F32)
            hj = jnp.square(jnp.maximum(hj, 0.0)).astype(BF16)
            f = f + jnp.dot(hj, wdn_ref[j * ff_block:(j + 1) * ff_block, :], preferred_element_type=F32)
        x2 = x1s[g] + mod_ref[0, 5:6, :] * f
        o_ref[0, g * sub:(g + 1) * sub, :] = _rms(x2) * fg_ref[...]


def _outmlp_call(x, yc, yr, mod, wo, wup, wdn, fg, *, tt):
    bsz, t, d = x.shape
    tok = lambda w: pl.BlockSpec((1, tt, w), lambda b, i: (b, i, 0))
    res = lambda a: pl.BlockSpec(a.shape, lambda b, i: (0,) * a.ndim, pipeline_mode=pl.Buffered(1))
    return pl.pallas_call(
        functools.partial(_outmlp_kernel, ff_block=FF_BLOCK, n_split=OUT_ROW_GROUPS),
        grid=(bsz, t // tt),
        in_specs=[tok(d), tok(yc.shape[2]), tok(yr.shape[2]),
                  pl.BlockSpec((1,) + mod.shape[1:], lambda b, i: (b, 0, 0)),
                  res(wo), res(wup), res(wdn), res(fg)],
        out_specs=tok(d),
        out_shape=jax.ShapeDtypeStruct((bsz, t, d), F32),
        compiler_params=pltpu.CompilerParams(dimension_semantics=("arbitrary", "arbitrary"),
                                             vmem_limit_bytes=VMEM_LIMIT),
        name="outmlp",
    )(x, yc, yr, mod, wo, wup, wdn, fg)


def kernel(x, c, w_ada, b_ada, w_in, conv_w, rwkv_mu, w0, w2, a0, a2, g2, k_k, k_a, r_k,
           lnx_g, lnx_b, w_out, w_up, w_down, final_g):
    assert w_ada.shape[0] == 1, "single-layer block"
    bsz, t, d = x.shape
    rw = w0.shape[1]
    n_mod = w_ada.shape[2] // d
    assert t % MIX_TILE == 0 and t % OUT_TILE == 0 and MIX_TILE % CHUNK == 0 and OUT_TILE % OUT_ROW_GROUPS == 0
    assert rw % QUAD == 0 and r_k.shape[-1] == HEAD and w_ada.shape[2] % MOD_BLOCK == 0
    assert w2.shape[1] + a2.shape[1] == PAIR and w_up.shape[2] % FF_BLOCK == 0

    mod = _mod_call(c, w_ada[0], b_ada).reshape(bsz, n_mod, d)

    zpad = jnp.zeros((PAIR - w2.shape[1], rw), F32)
    ww = jnp.concatenate([w2[0], zpad], axis=0).astype(BF16)
    wa = jnp.concatenate([zpad, a2[0]], axis=0).astype(BF16)
    pv = jnp.concatenate([w0, a0, k_k, k_a, r_k.reshape(1, rw), jnp.zeros((3, rw), F32)], axis=0)
    hid = jnp.arange(QUAD, dtype=jnp.int32) // HEAD
    bd = (hid[:, None] == hid[None, :]).astype(BF16)
    lnx = jnp.concatenate([lnx_g, lnx_b], axis=0)

    yc, yr = _mix_call(x, mod, w_in[0].astype(BF16), conv_w[0, :, 0, :], rwkv_mu, pv, ww, wa,
                       g2[0].astype(BF16), bd, lnx, tt=MIX_TILE)
    return _outmlp_call(x, yc, yr, mod, w_out[0].astype(BF16), w_up[0].astype(BF16),
                        w_down[0].astype(BF16), final_g.reshape(1, d), tt=OUT_TILE)
```

```python
import functools

import jax
import jax.numpy as jnp
from jax import lax
from jax.experimental import pallas as pl
from jax.experimental.pallas import tpu as pltpu

F32 = jnp.float32
BF16 = jnp.bfloat16

HEAD = 64
CHUNK = 64
PAIR = 2 * HEAD
QUAD = 4 * HEAD
EPS = 1e-6
LNX_EPS = 1e-5 * HEAD
NEG_EXP_M05 = -0.6065306597126334
VMEM_LIMIT = 56 * 1024 * 1024

MOD_BLOCK = 1024
MIX_TILE = 512
OUT_TILE = 1024
OUT_ROW_GROUPS = 4
FF_BLOCK = 1024


def _dot(a, b):
    return jnp.dot(a.astype(BF16), b.astype(BF16), preferred_element_type=F32)


def _dot_f32(a, b):
    return jnp.dot(a, b, precision=lax.Precision.HIGHEST, preferred_element_type=F32)


def _split2(x):
    hi = x.astype(BF16)
    lo = (x - hi.astype(F32)).astype(BF16)
    return hi, lo


def _quads_to_rows(x):
    return jnp.concatenate([x[:, i:i + QUAD] for i in range(0, x.shape[1], QUAD)], axis=0)


def _rows_to_quads(x, t):
    return jnp.concatenate([x[i:i + t] for i in range(0, x.shape[0], t)], axis=1)


def _headsum1(x, bd):
    s = jnp.dot(_quads_to_rows(x.astype(BF16)), bd, preferred_element_type=F32)
    return _rows_to_quads(s, x.shape[0])


def _rms(x):
    return x * lax.rsqrt(jnp.mean(x * x, axis=-1, keepdims=True) + EPS)


def _mod_kernel(c_ref, w_ref, b_ref, o_ref):
    c = c_ref[...]
    cond = c * jax.nn.sigmoid(c)
    o_ref[...] = _dot_f32(cond, w_ref[...]) + b_ref[...]


def _mod_call(c, w_ada, b_ada):
    bsz, d = c.shape
    n = w_ada.shape[1]
    bn = MOD_BLOCK
    return pl.pallas_call(
        _mod_kernel,
        grid=(n // bn,),
        in_specs=[pl.BlockSpec((bsz, d), lambda j: (0, 0)),
                  pl.BlockSpec((d, bn), lambda j: (0, j)),
                  pl.BlockSpec((1, bn), lambda j: (0, j))],
        out_specs=pl.BlockSpec((bsz, bn), lambda j: (0, j)),
        out_shape=jax.ShapeDtypeStruct((bsz, n), F32),
        compiler_params=pltpu.CompilerParams(dimension_semantics=("arbitrary",),
                                             vmem_limit_bytes=VMEM_LIMIT),
        name="mod",
    )(c, w_ada, b_ada)


SV_RH, SV_YH, SV_BW, SV_NT, SV_GC = range(5)
AB_AT, AB_V = range(2)
GT_BT, GT_KT = range(2)


def _mix_kernel(x_ref, mod_ref, win_ref, cw_ref, mu_ref, pv_ref, ww_ref, wa_ref, g2_ref, bd_ref, lnx_ref,
                yc_ref, yr_ref,
                zc_ref, qc_ref, p_scr, y_scr, sv_ref, bg_ref, ab_ref, rt_ref, gc_ref, gt_ref, tc_ref,
                *, conv_w, rw, n_tiles, tiles_per_seq):
    tt = x_ref.shape[1]
    step = pl.program_id(0)
    seq_start = lax.rem(step, jnp.int32(tiles_per_seq)) == 0
    chain_seq_start = lax.rem(step - 2, jnp.int32(tiles_per_seq)) == 0
    nquad = rw // QUAD
    nheads = QUAD // HEAD
    nchunk = tt // CHUNK

    @pl.when(step == 0)
    def _():
        for ref in (zc_ref, qc_ref, p_scr, sv_ref, bg_ref, ab_ref, rt_ref, gc_ref, gt_ref, tc_ref):
            ref[...] = jnp.zeros_like(ref)

    bd = bd_ref[...]
    ti = lax.broadcasted_iota(jnp.int32, (CHUNK, QUAD), 0)
    li = lax.broadcasted_iota(jnp.int32, (CHUNK, QUAD), 1)
    strict = ti > (li % HEAD)
    incl = ti >= (li % HEAD)
    eye_cat = jnp.where(ti == (li % HEAD), 1.0, 0.0).astype(F32)
    head_of_lane = li // HEAD
    qi = lax.broadcasted_iota(jnp.int32, (QUAD, QUAD), 0)
    qj = lax.broadcasted_iota(jnp.int32, (QUAD, QUAD), 1)
    eye_bd = qi == qj

    def bd_of(m):
        mb = m.astype(BF16)
        zero = jnp.zeros_like(mb)
        return jnp.concatenate([jnp.where(head_of_lane == hh, mb, zero) for hh in range(nheads)], axis=0)

    def col_to_cat(col):
        out = col[0:HEAD]
        for hh in range(1, nheads):
            out = jnp.where(head_of_lane == hh, col[hh * HEAD:(hh + 1) * HEAD], out)
        return out

    units = [(c, qd) for c in range(nchunk) for qd in range(nquad)]

    def rows(m, i):
        return m[i * CHUNK:(i + 1) * CHUNK]

    def ab(kind, u):
        c, qd = u
        return ab_ref[kind, c * CHUNK:(c + 1) * CHUNK, qd * QUAD:(qd + 1) * QUAD]

    def rt_prev(u):
        c, qd = u
        return rt_ref[c * CHUNK:(c + 1) * CHUNK, qd * QUAD:(qd + 1) * QUAD]

    state = [jnp.where(chain_seq_start, 0.0, p_scr[qd]) for qd in range(nquad)]
    pending = list(range(nchunk))

    def chain_step():
        if not pending:
            return
        c = pending.pop(0)
        for qd in range(nquad):
            i = c * nquad + qd
            pp = state[qd]
            m = _dot(jnp.concatenate([sv_ref[SV_RH, i], sv_ref[SV_BW, i]], axis=0), bd_of(pp))
            y_scr[c * CHUNK:(c + 1) * CHUNK, qd * QUAD:(qd + 1) * QUAD] = rows(m, 0) + sv_ref[SV_YH, i]
            state[qd] = sv_ref[SV_GC, i] * pp + rows(m, 1) + sv_ref[SV_NT, i]

    x = x_ref[0]
    h = _rms(x) * (1.0 + mod_ref[0, 1:2, :]) + mod_ref[0, 0:1, :]
    p = jnp.dot(h.astype(BF16), win_ref[...], preferred_element_type=F32)
    first = lax.broadcasted_iota(jnp.int32, (8, 1), 0) == 0
    cw = conv_w

    def shift_rows(m, carry_row):
        rolled = pltpu.roll(m, 1, 0)
        return jnp.concatenate([jnp.where(first, carry_row, rolled[0:8]), rolled[8:]], axis=0)

    at_prev = [ab(AB_AT, u) for u in units]
    v_prev = [ab(AB_V, u) for u in units]
    rt_prevs = [rt_prev(u) for u in units]
    g_cats = [col_to_cat(jnp.sum(jnp.where(eye_bd, gc_ref[c:c + 1, qd * QUAD:(qd + 1) * QUAD], 0.0),
                                 axis=1, keepdims=True)) for c, qd in units]
    gram = [_dot(jnp.concatenate([at_prev[i], rt_prevs[i].astype(BF16)], axis=0),
                 jnp.concatenate([gt_ref[i, GT_BT], gt_ref[i, GT_KT]], axis=1))
            for i in range(len(units))]
    chain_step()

    z = p[:, cw:2 * cw] * p[:, 2 * cw:3 * cw]
    z1 = shift_rows(z, jnp.where(seq_start, 0.0, zc_ref[0:1, :]))
    z2 = shift_rows(z1, jnp.where(seq_start, 0.0, zc_ref[1:2, :]))
    zc_ref[0:1, :] = z[tt - 1:tt, :]
    zc_ref[1:2, :] = z1[tt - 1:tt, :]
    yc = (p[:, 0:cw] * (cw_ref[0:1, :] * z2 + cw_ref[1:2, :] * z1 + cw_ref[2:3, :] * z)).astype(BF16)

    aab = [jnp.where(strict, rows(m, 0)[:, 0:QUAD], 0.0) for m in gram]
    aak = [jnp.where(strict, rows(m, 0)[:, QUAD:], 0.0) for m in gram]
    rb = [jnp.where(incl, rows(m, 1)[:, 0:QUAD], 0.0) for m in gram]
    rk = [jnp.where(incl, rows(m, 1)[:, QUAD:], 0.0) for m in gram]
    btt = [tc_ref[i, GT_BT].astype(BF16) for i in range(len(units))]
    ktt = [tc_ref[i, GT_KT].astype(BF16) for i in range(len(units))]
    tinv = [eye_cat + m for m in aab]
    apow = [_dot(m, bd_of(m)) for m in aab]

    q = p[:, 3 * cw:]
    qprev = shift_rows(q, jnp.where(seq_start, 0.0, qc_ref[0:1, :]))
    qc_ref[0:1, :] = q[tt - 1:tt, :]
    q = q + (qprev - q) * mu_ref[...]
    r = q[:, 0:rw]
    k = q[:, rw:2 * rw]
    v_new = q[:, 2 * rw:3 * rw]
    wa = q[:, 3 * rw:3 * rw + PAIR]
    gl = q[:, 3 * rw + PAIR:]
    lane = lax.broadcasted_iota(jnp.int32, (1, PAIR), 1)
    wa = jnp.where(lane < HEAD, jnp.tanh(wa), wa)
    w_pre = _dot(wa, ww_ref[...])
    a_pre = _dot(wa, wa_ref[...])
    gate = _dot(jax.nn.sigmoid(gl), g2_ref[...])
    w0, a0, k_k, k_a, r_k = (pv_ref[i:i + 1, :] for i in range(5))

    def level(apow, tinv):
        sq = [_dot(jnp.concatenate([m, t], axis=0), bd_of(m)) for m, t in zip(apow, tinv)]
        return [rows(m, 0) for m in sq], [t + rows(m, 1) for t, m in zip(tinv, sq)]

    n_levels = 0
    while (4 << n_levels) < CHUNK:
        n_levels += 1
    for _ in range(n_levels // 2):
        apow, tinv = level(apow, tinv)
        chain_step()

    ld = jax.nn.sigmoid(w0 + w_pre) * NEG_EXP_M05
    a = jax.nn.sigmoid(a0 + a_pre)
    kk = k * k_k
    kk = kk / jnp.maximum(jnp.sqrt(_headsum1(kk * kk, bd)), 1e-12)
    k2 = k * (1.0 + (a - 1.0) * k_a)
    b = kk * a
    bonus = _headsum1(r * k2 * r_k, bd) * v_new

    for _ in range(n_levels - n_levels // 2):
        apow, tinv = level(apow, tinv)
        chain_step()
    tinv = [t + _dot(t, bd_of(m)) for m, t in zip(apow, tinv)]

    ri = lax.broadcasted_iota(jnp.int32, (tt, tt), 0)
    ci = lax.broadcasted_iota(jnp.int32, (tt, tt), 1)
    tril = jnp.where(((ri // CHUNK) == (ci // CHUNK)) & (ri >= ci), 1.0, 0.0).astype(BF16)
    cum = sum(jnp.dot(tril, part, preferred_element_type=F32) for part in _split2(ld))
    last = [cum[c * CHUNK + CHUNK - 1:(c + 1) * CHUNK, :] for c in range(nchunk)]
    einv = jnp.exp(-cum)

    vres = [_dot(jnp.concatenate([m1.astype(BF16), m2.astype(BF16), m3], axis=0), bd_of(v))
            for m1, m2, m3, v in zip(aak, rk, ktt, v_prev)]
    while pending:
        chain_step()

    bt_new = b * einv
    kt_new = k2 * einv
    rt_ref[...] = r * jnp.exp(cum)
    ab_ref[AB_AT] = (-kk * jnp.exp(cum - ld)).astype(BF16)
    ab_ref[AB_V] = v_new.astype(BF16)
    for c in range(nchunk):
        gc_ref[c:c + 1, :] = jnp.exp(last[c])
    slot_in = lax.rem(step, jnp.int32(3))
    bg_ref[slot_in, 0] = bonus
    bg_ref[slot_in, 1] = gate
    for i, (c, qd) in enumerate(units):
        for kind, full_tile in ((GT_BT, bt_new), (GT_KT, kt_new)):
            xt = jnp.transpose(full_tile[c * CHUNK:(c + 1) * CHUNK, qd * QUAD:(qd + 1) * QUAD])
            for hh in range(nheads):
                blk_t = xt[hh * HEAD:(hh + 1) * HEAD, :]
                gt_ref[i, kind, hh * HEAD:(hh + 1) * HEAD, hh * CHUNK:(hh + 1) * CHUNK] = blk_t
                tc_ref[i, kind, :, hh * CHUNK:(hh + 1) * CHUNK] = blk_t

    lt = [_dot(jnp.concatenate([m, bt], axis=0), bd_of(t)) for m, bt, t in zip(rb, btt, tinv)]

    for qd in range(nquad):
        p_scr[qd] = state[qd]
    slot_out = lax.rem(step + 1, jnp.int32(3))
    y = y_scr[...]
    d = y - _headsum1(y, bd) * (1.0 / HEAD)
    var = _headsum1(d * d, bd) * (1.0 / HEAD)
    yn = d * lax.rsqrt(var + LNX_EPS)
    yn = yn * lnx_ref[0:1, :] + lnx_ref[1:2, :] + bg_ref[slot_out, 0]
    yr_ref[0] = (yn * bg_ref[slot_out, 1]).astype(BF16)

    res = [_dot(m, jnp.concatenate([bd_of(a_), bd_of(rows(x, 0))], axis=1))
           for m, x, a_ in zip(lt, vres, at_prev)]
    for i in range(len(units)):
        g_cat = g_cats[i]
        sv_ref[SV_RH, i] = rt_prevs[i] + rows(res[i], 0)[:, 0:QUAD]
        sv_ref[SV_YH, i] = rows(res[i], 0)[:, QUAD:] + rows(vres[i], 1)
        sv_ref[SV_BW, i] = g_cat * rows(res[i], 1)[:, 0:QUAD]
        sv_ref[SV_NT, i] = g_cat * (rows(res[i], 1)[:, QUAD:] + rows(vres[i], 2))
        sv_ref[SV_GC, i] = g_cat

    @pl.when(step < n_tiles)
    def _():
        yc_ref[0] = yc


def _mix_call(x, mod, win, cw, mu, pv, ww, wa, g2, bd, lnx, *, tt):
    bsz, t, d = x.shape
    ptot = win.shape[1]
    conv_w = cw.shape[1]
    rw = pv.shape[1]
    per_seq = t // tt
    n_tiles = bsz * per_seq
    assert tt // CHUNK <= 8, "gc scratch holds one row per chunk"

    def tile_at(lag):
        def index_map(i):
            j = jnp.clip(i - lag, 0, n_tiles - 1)
            return (j // per_seq, j % per_seq, 0)
        return index_map

    full = lambda a: pl.BlockSpec(a.shape, lambda i: (0,) * a.ndim)
    cur = lambda w: pl.BlockSpec((1, tt, w), tile_at(0))
    lag2 = lambda w: pl.BlockSpec((1, tt, w), tile_at(2))
    n_units = (tt // CHUNK) * (rw // QUAD)
    return pl.pallas_call(
        functools.partial(_mix_kernel, conv_w=conv_w, rw=rw, n_tiles=n_tiles, tiles_per_seq=per_seq),
        grid=(n_tiles + 2,),
        in_specs=[cur(d),
                  pl.BlockSpec((1,) + mod.shape[1:], lambda i: (jnp.minimum(i, n_tiles - 1) // per_seq, 0, 0)),
                  full(win), full(cw), full(mu), full(pv), full(ww), full(wa), full(g2), full(bd),
                  full(lnx)],
        out_specs=[cur(conv_w), lag2(rw)],
        out_shape=[jax.ShapeDtypeStruct((bsz, t, conv_w), BF16), jax.ShapeDtypeStruct((bsz, t, rw), BF16)],
        scratch_shapes=[pltpu.VMEM((8, conv_w), F32),
                        pltpu.VMEM((8, ptot - 3 * conv_w), F32),
                        pltpu.VMEM((rw // QUAD, HEAD, QUAD), F32),
                        pltpu.VMEM((tt, rw), F32),
                        pltpu.VMEM((5, n_units, CHUNK, QUAD), F32),
                        pltpu.VMEM((3, 2, tt, rw), F32),
                        pltpu.VMEM((2, tt, rw), BF16),
                        pltpu.VMEM((tt, rw), F32),
                        pltpu.VMEM((8, rw), F32),
                        pltpu.VMEM((n_units, 2, QUAD, QUAD), F32),
                        pltpu.VMEM((n_units, 2, CHUNK, QUAD), F32)],
        compiler_params=pltpu.CompilerParams(dimension_semantics=("arbitrary",),
                                             vmem_limit_bytes=VMEM_LIMIT,
                                             allow_input_fusion=[False, False, True] + [False] * 8),
        name="mix",
    )(x, mod, win, cw, mu, pv, ww, wa, g2, bd, lnx)


def _outmlp_kernel(x_ref, yc_ref, yr_ref, mod_ref, wo_ref, wup_ref, wdn_ref, fg_ref, o_ref,
                   *, ff_block, n_split):
    cw = yc_ref.shape[2]
    sub = x_ref.shape[1] // n_split
    x1s, hs = [], []
    for g in range(n_split):
        rs = slice(g * sub, (g + 1) * sub)
        y = (jnp.dot(yc_ref[0, rs, :], wo_ref[0:cw, :], preferred_element_type=F32)
             + jnp.dot(yr_ref[0, rs, :], wo_ref[cw:, :], preferred_element_type=F32))
        x1 = x_ref[0, rs, :] + mod_ref[0, 2:3, :] * y
        x1s.append(x1)
        hs.append((_rms(x1) * (1.0 + mod_ref[0, 4:5, :]) + mod_ref[0, 3:4, :]).astype(BF16))
    for g in range(n_split):
        f = jnp.zeros_like(x1s[g])
        for j in range(wup_ref.shape[1] // ff_block):
            hj = jnp.dot(hs[g], wup_ref[:, j * ff_block:(j + 1) * ff_block], preferred_element_type=F32)
            hj = jnp.square(jnp.maximum(hj, 0.0)).astype(BF16)
            f = f + jnp.dot(hj, wdn_ref[j * ff_block:(j + 1) * ff_block, :], preferred_element_type=F32)
        x2 = x1s[g] + mod_ref[0, 5:6, :] * f
        o_ref[0, g * sub:(g + 1) * sub, :] = _rms(x2) * fg_ref[...]


def _outmlp_call(x, yc, yr, mod, wo, wup, wdn, fg, *, tt):
    bsz, t, d = x.shape
    tok = lambda w: pl.BlockSpec((1, tt, w), lambda b, i: (b, i, 0))
    res = lambda a: pl.BlockSpec(a.shape, lambda b, i: (0,) * a.ndim, pipeline_mode=pl.Buffered(1))
    return pl.pallas_call(
        functools.partial(_outmlp_kernel, ff_block=FF_BLOCK, n_split=OUT_ROW_GROUPS),
        grid=(bsz, t // tt),
        in_specs=[tok(d), tok(yc.shape[2]), tok(yr.shape[2]),
                  pl.BlockSpec((1,) + mod.shape[1:], lambda b, i: (b, 0, 0)),
                  res(wo), res(wup), res(wdn), res(fg)],
        out_specs=tok(d),
        out_shape=jax.ShapeDtypeStruct((bsz, t, d), F32),
        compiler_params=pltpu.CompilerParams(dimension_semantics=("arbitrary", "arbitrary"),
                                             vmem_limit_bytes=VMEM_LIMIT,
                                             allow_input_fusion=[False] * 4 + [True] * 3 + [False]),
        name="outmlp",
    )(x, yc, yr, mod, wo, wup, wdn, fg)


def kernel(x, c, w_ada, b_ada, w_in, conv_w, rwkv_mu, w0, w2, a0, a2, g2, k_k, k_a, r_k,
           lnx_g, lnx_b, w_out, w_up, w_down, final_g):
    assert w_ada.shape[0] == 1, "single-layer block"
    bsz, t, d = x.shape
    rw = w0.shape[1]
    n_mod = w_ada.shape[2] // d
    assert t % MIX_TILE == 0 and t % OUT_TILE == 0 and MIX_TILE % CHUNK == 0 and OUT_TILE % OUT_ROW_GROUPS == 0
    assert rw % QUAD == 0 and r_k.shape[-1] == HEAD and w_ada.shape[2] % MOD_BLOCK == 0
    assert w2.shape[1] + a2.shape[1] == PAIR and w_up.shape[2] % FF_BLOCK == 0

    mod = _mod_call(c, w_ada[0], b_ada).reshape(bsz, n_mod, d)

    zpad = jnp.zeros((PAIR - w2.shape[1], rw), F32)
    ww = jnp.concatenate([w2[0], zpad], axis=0).astype(BF16)
    wa = jnp.concatenate([zpad, a2[0]], axis=0).astype(BF16)
    pv = jnp.concatenate([w0, a0, k_k, k_a, r_k.reshape(1, rw), jnp.zeros((3, rw), F32)], axis=0)
    hid = jnp.arange(QUAD, dtype=jnp.int32) // HEAD
    bd = (hid[:, None] == hid[None, :]).astype(BF16)
    lnx = jnp.concatenate([lnx_g, lnx_b], axis=0)

    yc, yr = _mix_call(x, mod, w_in[0].astype(BF16), conv_w[0, :, 0, :], rwkv_mu, pv, ww, wa,
                       g2[0].astype(BF16), bd, lnx, tt=MIX_TILE)
    return _outmlp_call(x, yc, yr, mod, w_out[0].astype(BF16), w_up[0].astype(BF16),
                        w_down[0].astype(BF16), final_g.reshape(1, d), tt=OUT_TILE)
```
